```python
import math
import jax, jax.numpy as jnp
from jax import lax
import numpy as np


D_MODEL = 2048
BATCH = 4
SEQ = 8192
DEPTH = 1
DEC_BATCH = 1
DEC_SEQ = 16384
PAST_LEN = 128

N_META = 16
ROPE_THETA = 500000.0
Q_BLOCK = 128
DA_HEADS = 8
DA_HEAD_DIM = 64
DA_V_DIM = 2 * DA_HEAD_DIM
DA_ROT = DA_HEAD_DIM // 4
DA_WIDTH = DA_HEADS * DA_V_DIM
MLA_HEADS = 8
MLA_Q_RANK = 512
MLA_KV_RANK = 256
MLA_NOPE = 128
MLA_ROPE = 64
MLA_V = 128
MLA_WIDTH = MLA_HEADS * MLA_V
MIX_WIDTH = DA_WIDTH + MLA_WIDTH
D_FF = 5632
DA_QK_COLS = DA_HEADS * 2 * DA_HEAD_DIM
SPLITS = [DA_QK_COLS, 2 * DA_QK_COLS, 2 * DA_QK_COLS + DA_WIDTH,
          2 * DA_QK_COLS + DA_WIDTH + MLA_Q_RANK,
          2 * DA_QK_COLS + DA_WIDTH + MLA_Q_RANK + MLA_KV_RANK]
IN_COLS = 2 * DA_QK_COLS + DA_WIDTH + MLA_Q_RANK + MLA_KV_RANK + MLA_ROPE
ALPHA = (2 * DEPTH) ** 0.25
BETA = (8 * DEPTH) ** -0.25
LN_EPS = 1e-5
RMS_EPS = 1e-6

kernel_name = 'hymba_diffattn_mla_deepnorm_encoder'


def layer_norm(x, g, b):
    xf = x.astype(jnp.float32)
    mu = jnp.mean(xf, axis=-1, keepdims=True)
    var = jnp.mean(jnp.square(xf - mu), axis=-1, keepdims=True)
    y = (xf - mu) * lax.rsqrt(var + LN_EPS) * g.astype(jnp.float32) + b.astype(jnp.float32)
    return y.astype(x.dtype)


def rms_norm(x, g):
    xf = x.astype(jnp.float32)
    y = xf * lax.rsqrt(jnp.mean(jnp.square(xf), axis=-1, keepdims=True) + RMS_EPS)
    return (y * g.astype(jnp.float32)).astype(x.dtype)


def rope_tables(n_pos, rot_dim):
    inv = 1.0 / (ROPE_THETA ** (jnp.arange(0, rot_dim, 2, dtype=jnp.float32) / rot_dim))
    ang = jnp.arange(n_pos, dtype=jnp.float32)[:, None] * inv[None, :]
    return jnp.cos(ang), jnp.sin(ang)


def apply_rope(x, cos, sin, rot_dim):
    half = rot_dim // 2
    shp = (cos.shape[0],) + (1,) * (x.ndim - 3) + (half,)
    c = cos.reshape(shp)
    s = sin.reshape(shp)
    xf = x.astype(jnp.float32)
    x1 = xf[..., :half]
    x2 = xf[..., half:rot_dim]
    out = jnp.concatenate([x1 * c - x2 * s, x2 * c + x1 * s, xf[..., rot_dim:]], axis=-1)
    return out.astype(x.dtype)


def to_query_blocks(q):
    B, L = q.shape[0], q.shape[1]
    pad = (-L) % Q_BLOCK
    q = jnp.pad(q, ((0, 0), (0, pad)) + ((0, 0),) * (q.ndim - 2))
    nb = q.shape[1] // Q_BLOCK
    return jnp.moveaxis(q.reshape((B, nb, Q_BLOCK) + q.shape[2:]), 1, 0)


def from_query_blocks(o, L):
    nb, B = o.shape[0], o.shape[1]
    o = jnp.moveaxis(o, 0, 1).reshape((B, nb * Q_BLOCK) + o.shape[3:])
    return o[:, :L]


def diff_attention(q, k, v, lam):
    L = q.shape[1]
    scale = q.shape[-1] ** -0.5

    def block(qb):
        s = jnp.einsum('bqchd,bkchd->bchqk', qb, k).astype(jnp.float32) * scale
        p = jax.nn.softmax(s, axis=-1)
        w = p[:, 0] - lam * p[:, 1]
        return jnp.einsum('bhqk,bkhe->bqhe', w.astype(v.dtype), v)

    return from_query_blocks(lax.map(block, to_query_blocks(q)), L)


def mla_attention(q, k, v):
    L = q.shape[1]
    scale = q.shape[-1] ** -0.5

    def block(qb):
        s = jnp.einsum('bqhd,bkhd->bhqk', qb, k).astype(jnp.float32) * scale
        p = jax.nn.softmax(s, axis=-1)
        return jnp.einsum('bhqk,bkhe->bqhe', p.astype(v.dtype), v)

    return from_query_blocks(lax.map(block, to_query_blocks(q)), L)


def swiglu(x, w_gate, w_up, w_down):
    return (jax.nn.silu(x @ w_gate) * (x @ w_up)) @ w_down


def token_mixers(h, p, i, rope_da, rope_mla):
    B, L, _ = h.shape
    z = h @ p['w_in'][i]
    zq, zk, zv, zcq, zckv, zkr = jnp.split(z, SPLITS, axis=-1)

    cos_a, sin_a = rope_da
    qa = jnp.swapaxes(zq.reshape(B, L, DA_HEADS, 2, DA_HEAD_DIM), 2, 3)
    ka = jnp.swapaxes(zk.reshape(B, L, DA_HEADS, 2, DA_HEAD_DIM), 2, 3)
    qa = apply_rope(qa, cos_a, sin_a, DA_ROT)
    ka = apply_rope(ka, cos_a, sin_a, DA_ROT)
    va = zv.reshape(B, L, DA_HEADS, DA_V_DIM)
    lambda_init = 0.8 - 0.6 * math.exp(-0.3 * i)
    f32 = jnp.float32
    lam = (jnp.exp(jnp.sum(p['da_lambda_q1'][i].astype(f32) * p['da_lambda_k1'][i].astype(f32)))
           - jnp.exp(jnp.sum(p['da_lambda_q2'][i].astype(f32) * p['da_lambda_k2'][i].astype(f32)))
           + lambda_init)
    oa = diff_attention(qa, ka, va, lam)
    oa = rms_norm(oa, p['da_subln_g'][i]) * (1.0 - lambda_init)
    oa = oa.reshape(B, L, DA_WIDTH)

    cos_m, sin_m = rope_mla
    cq = rms_norm(zcq, p['mla_q_norm_g'][i])
    qm = (cq @ p['w_uq'][i]).reshape(B, L, MLA_HEADS, MLA_NOPE + MLA_ROPE)
    q_rope = apply_rope(qm[..., MLA_NOPE:], cos_m, sin_m, MLA_ROPE)
    qm = jnp.concatenate([qm[..., :MLA_NOPE], q_rope], axis=-1)
    ckv = rms_norm(zckv, p['mla_kv_norm_g'][i])
    kv = (ckv @ p['w_ukv'][i]).reshape(B, L, MLA_HEADS, MLA_NOPE + MLA_V)
    k_rope = apply_rope(zkr[:, :, None, :], cos_m, sin_m, MLA_ROPE)
    km = jnp.concatenate([kv[..., :MLA_NOPE],
                          jnp.broadcast_to(k_rope, (B, L, MLA_HEADS, MLA_ROPE))], axis=-1)
    vm = kv[..., MLA_NOPE:]
    om = mla_attention(qm, km, vm).reshape(B, L, MLA_WIDTH)

    return jnp.concatenate([oa, om], axis=-1) @ p['w_o'][i]


def encoder_trunk(x, meta_tokens, p):
    B, S, D = x.shape
    meta = jnp.broadcast_to(meta_tokens[None].astype(x.dtype), (B, N_META, D))
    h = jnp.concatenate([meta, x], axis=1)
    L = h.shape[1]
    rope_da = rope_tables(L, DA_ROT)
    rope_mla = rope_tables(L, MLA_ROPE)
    for i in range(DEPTH):
        h = layer_norm(ALPHA * h + 0.5 * swiglu(h, p['ffn1_w_gate'][i], p['ffn1_w_up'][i], p['ffn1_w_down'][i]),
                       p['ln1_g'][i], p['ln1_b'][i])
        h = layer_norm(ALPHA * h + token_mixers(h, p, i, rope_da, rope_mla),
                       p['ln2_g'][i], p['ln2_b'][i])
        h = layer_norm(ALPHA * h + 0.5 * swiglu(h, p['ffn2_w_gate'][i], p['ffn2_w_up'][i], p['ffn2_w_down'][i]),
                       p['ln3_g'][i], p['ln3_b'][i])
    return h[:, N_META:]


def setup_inputs(seed: int = 0) -> dict:
    key = jax.random.key(seed)
    ks = iter(jax.random.split(key, 40))

    def nrm(shape, scale):
        return jax.random.normal(next(ks), shape, jnp.float32) * scale

    def gain(shape):
        return 1.0 + nrm(shape, 0.02)

    def bias(shape):
        return nrm(shape, 0.02)

    Dp = DEPTH
    return {
        'x_prompt': nrm((BATCH, SEQ, D_MODEL), 1.0),
        'x_sample': nrm((DEC_BATCH, DEC_SEQ, D_MODEL), 1.0),
        'meta_tokens': nrm((N_META, D_MODEL), 1.0),
        'ffn1_w_gate': nrm((Dp, D_MODEL, D_FF), D_MODEL ** -0.5),
        'ffn1_w_up': nrm((Dp, D_MODEL, D_FF), D_MODEL ** -0.5),
        'ffn1_w_down': nrm((Dp, D_FF, D_MODEL), BETA * D_FF ** -0.5),
        'ln1_g': gain((Dp, D_MODEL)),
        'ln1_b': bias((Dp, D_MODEL)),
        'w_in': nrm((Dp, D_MODEL, IN_COLS), D_MODEL ** -0.5),
        'da_lambda_q1': nrm((Dp, DA_HEAD_DIM), 0.1),
        'da_lambda_k1': nrm((Dp, DA_HEAD_DIM), 0.1),
        'da_lambda_q2': nrm((Dp, DA_HEAD_DIM), 0.1),
        'da_lambda_k2': nrm((Dp, DA_HEAD_DIM), 0.1),
        'da_subln_g': gain((Dp, DA_V_DIM)),
        'mla_q_norm_g': gain((Dp, MLA_Q_RANK)),
        'w_uq': nrm((Dp, MLA_Q_RANK, MLA_HEADS * (MLA_NOPE + MLA_ROPE)), MLA_Q_RANK ** -0.5),
        'mla_kv_norm_g': gain((Dp, MLA_KV_RANK)),
        'w_ukv': nrm((Dp, MLA_KV_RANK, MLA_HEADS * (MLA_NOPE + MLA_V)), MLA_KV_RANK ** -0.5),
        'w_o': nrm((Dp, MIX_WIDTH, D_MODEL), BETA * MIX_WIDTH ** -0.5),
        'ln2_g': gain((Dp, D_MODEL)),
        'ln2_b': bias((Dp, D_MODEL)),
        'ffn2_w_gate': nrm((Dp, D_MODEL, D_FF), D_MODEL ** -0.5),
        'ffn2_w_up': nrm((Dp, D_MODEL, D_FF), D_MODEL ** -0.5),
        'ffn2_w_down': nrm((Dp, D_FF, D_MODEL), BETA * D_FF ** -0.5),
        'ln3_g': gain((Dp, D_MODEL)),
        'ln3_b': bias((Dp, D_MODEL)),
    }


def reference(x_prompt, x_sample, meta_tokens,
              ffn1_w_gate, ffn1_w_up, ffn1_w_down, ln1_g, ln1_b,
              w_in, da_lambda_q1, da_lambda_k1, da_lambda_q2, da_lambda_k2, da_subln_g,
              mla_q_norm_g, w_uq, mla_kv_norm_g, w_ukv, w_o, ln2_g, ln2_b,
              ffn2_w_gate, ffn2_w_up, ffn2_w_down, ln3_g, ln3_b):
    p = {
        'ffn1_w_gate': ffn1_w_gate, 'ffn1_w_up': ffn1_w_up, 'ffn1_w_down': ffn1_w_down,
        'ln1_g': ln1_g, 'ln1_b': ln1_b,
        'w_in': w_in,
        'da_lambda_q1': da_lambda_q1, 'da_lambda_k1': da_lambda_k1,
        'da_lambda_q2': da_lambda_q2, 'da_lambda_k2': da_lambda_k2,
        'da_subln_g': da_subln_g,
        'mla_q_norm_g': mla_q_norm_g, 'w_uq': w_uq,
        'mla_kv_norm_g': mla_kv_norm_g, 'w_ukv': w_ukv,
        'w_o': w_o, 'ln2_g': ln2_g, 'ln2_b': ln2_b,
        'ffn2_w_gate': ffn2_w_gate, 'ffn2_w_up': ffn2_w_up, 'ffn2_w_down': ffn2_w_down,
        'ln3_g': ln3_g, 'ln3_b': ln3_b,
    }
    y_prompt = encoder_trunk(x_prompt, meta_tokens, p)
    y_sample = encoder_trunk(x_sample, meta_tokens, p)
    return (y_prompt, y_sample)
```

```python
import functools
import math

import jax
import jax.numpy as jnp
from jax import lax
from jax.experimental import pallas as pl
from jax.experimental.pallas import tpu as pltpu

N_META = 16
ROPE_THETA = 500000.0
DA_HEADS = 8
DA_HEAD_DIM = 64
DA_V_DIM = 2 * DA_HEAD_DIM
DA_ROT = DA_HEAD_DIM // 4
DA_QK_COLS = DA_HEADS * 2 * DA_HEAD_DIM
DA_WIDTH = DA_HEADS * DA_V_DIM
MLA_HEADS = 8
MLA_Q_RANK = 512
MLA_KV_RANK = 256
MLA_NOPE = 128
MLA_ROPE = 64
MLA_V = 128
MLA_WIDTH = MLA_HEADS * MLA_V
MLA_QK_PAD = 256
DEPTH = 1
ALPHA = (2 * DEPTH) ** 0.25
LN_EPS = 1e-5
RMS_EPS = 1e-6
LAMBDA_INIT = 0.8 - 0.6 * math.exp(-0.3 * 0)
LOG2E = 1.4426950408889634

LANES = 128
META_PAD = 128
NEG_BIG = -1e30

F32 = jnp.float32
BF16 = jnp.bfloat16
MIB = 1024 * 1024


def _cparams(n_axes, vmem_mib):
    return pltpu.CompilerParams(
        dimension_semantics=("arbitrary",) * n_axes,
        vmem_limit_bytes=int(vmem_mib * MIB),
    )


def _resident(shape):
    n = len(shape)
    return pl.BlockSpec(shape, lambda *_: (0,) * n, pipeline_mode=pl.Buffered(1))


def _layer_norm(y, g, b):
    mu = jnp.mean(y, axis=-1, keepdims=True)
    d = y - mu
    var = jnp.mean(d * d, axis=-1, keepdims=True)
    return d * lax.rsqrt(var + LN_EPS) * g + b


def _rms_norm(x, g):
    return x * lax.rsqrt(jnp.mean(x * x, axis=-1, keepdims=True) + RMS_EPS) * g


def _ffn_ln_kernel(x_ref, wg_ref, wu_ref, wd_ref, g_ref, b_ref, o_ref, xb_ref, acc_ref):
    j = pl.program_id(1)

    @pl.when(j == 0)
    def _():
        xb_ref[...] = x_ref[...].astype(BF16)
        acc_ref[...] = jnp.zeros_like(acc_ref)

    xb = xb_ref[...]
    gate = jnp.dot(xb, wg_ref[...], preferred_element_type=F32)
    up = jnp.dot(xb, wu_ref[...], preferred_element_type=F32)
    hid = (gate * (1.0 / (1.0 + jnp.exp(-gate)))) * up
    acc_ref[...] += jnp.dot(hid.astype(BF16), wd_ref[...], preferred_element_type=F32)

    @pl.when(j == pl.num_programs(1) - 1)
    def _():
        y = ALPHA * x_ref[...] + 0.5 * acc_ref[...]
        o_ref[...] = _layer_norm(y, g_ref[...], b_ref[...])


def _ffn_ln(x, wg, wu, wd, g, b, *, tm=512, tf=512):
    m, d = x.shape
    f = wg.shape[1]
    tm = min(tm, m)
    tf = min(tf, f)
    assert m % tm == 0 and f % tf == 0
    vmem = (2 * 2 * tm * d * 4 + 2 * 3 * d * tf * 2 + tm * d * 6 + 4 * tm * tf * 4) / MIB + 6
    return pl.pallas_call(
        _ffn_ln_kernel,
        grid=(m // tm, f // tf),
        in_specs=[
            pl.BlockSpec((tm, d), lambda i, j: (i, 0)),
            pl.BlockSpec((d, tf), lambda i, j: (0, j)),
            pl.BlockSpec((d, tf), lambda i, j: (0, j)),
            pl.BlockSpec((tf, d), lambda i, j: (j, 0)),
            pl.BlockSpec((1, d), lambda i, j: (0, 0)),
            pl.BlockSpec((1, d), lambda i, j: (0, 0)),
        ],
        out_specs=pl.BlockSpec((tm, d), lambda i, j: (i, 0)),
        out_shape=jax.ShapeDtypeStruct((m, d), F32),
        scratch_shapes=[pltpu.VMEM((tm, d), BF16), pltpu.VMEM((tm, d), F32)],
        compiler_params=_cparams(2, vmem),
        name="ffn_ln",
    )(x, wg, wu, wd, g, b)


def _rope(x, c, sp, sm, half):
    return x * c + pltpu.roll(x, half, 1) * sp + pltpu.roll(x, LANES - half, 1) * sm


def _inproj_kernel(h_ref, win_ref, gq_ref, wuq_ref, gkv_ref, wukv_ref,
                   cda_ref, spda_ref, smda_ref, cm_ref, spm_ref, smm_ref,
                   qda_ref, kda_ref, vda_ref, qm_ref, km_ref, vm_ref):
    tm = h_ref.shape[0]
    hb = h_ref[...].astype(BF16)

    def proj(lo, hi):
        return jnp.dot(hb, win_ref[:, lo:hi], preferred_element_type=F32)

    cda, spda, smda = cda_ref[...], spda_ref[...], smda_ref[...]
    cm, spm, smm = cm_ref[...], spm_ref[...], smm_ref[...]
    da_half = DA_ROT // 2
    mla_half = MLA_ROPE // 2
    first = lax.broadcasted_iota(jnp.int32, (tm, LANES), 1) < DA_HEAD_DIM
    da_scale = DA_HEAD_DIM ** -0.5

    c0 = 0
    zq = proj(c0, c0 + DA_QK_COLS)
    for h in range(DA_HEADS):
        r = _rope(zq[:, h * LANES:(h + 1) * LANES], cda, spda, smda, da_half) * da_scale
        qda_ref[0, h, 0] = jnp.where(first, r, 0.0).astype(BF16)
        qda_ref[0, h, 1] = jnp.where(first, 0.0, r).astype(BF16)
    c0 += DA_QK_COLS
    zk = proj(c0, c0 + DA_QK_COLS)
    for h in range(DA_HEADS):
        kda_ref[0, h] = _rope(zk[:, h * LANES:(h + 1) * LANES], cda, spda, smda, da_half).astype(BF16)
    c0 += DA_QK_COLS
    zv = proj(c0, c0 + DA_WIDTH)
    for h in range(DA_HEADS):
        vda_ref[0, h] = zv[:, h * DA_V_DIM:(h + 1) * DA_V_DIM].astype(BF16)
    c0 += DA_WIDTH

    cq = _rms_norm(proj(c0, c0 + MLA_Q_RANK), gq_ref[...]).astype(BF16)
    c0 += MLA_Q_RANK
    qm = jnp.dot(cq, wuq_ref[...], preferred_element_type=F32)
    for h in range(MLA_HEADS):
        base = h * MLA_QK_PAD
        qm_ref[0, h, :, 0:MLA_NOPE] = qm[:, base:base + MLA_NOPE].astype(BF16)
        qm_ref[0, h, :, MLA_NOPE:MLA_QK_PAD] = _rope(
            qm[:, base + MLA_NOPE:base + MLA_QK_PAD], cm, spm, smm, mla_half).astype(BF16)

    ckv = _rms_norm(proj(c0, c0 + MLA_KV_RANK), gkv_ref[...]).astype(BF16)
    c0 += MLA_KV_RANK
    kv = jnp.dot(ckv, wukv_ref[...], preferred_element_type=F32)
    kr = _rope(proj(c0, c0 + LANES), cm, spm, smm, mla_half).astype(BF16)
    for h in range(MLA_HEADS):
        base = h * (MLA_NOPE + MLA_V)
        km_ref[0, h, :, 0:MLA_NOPE] = kv[:, base:base + MLA_NOPE].astype(BF16)
        km_ref[0, h, :, MLA_NOPE:MLA_QK_PAD] = kr
        vm_ref[0, h] = kv[:, base + MLA_NOPE:base + MLA_NOPE + MLA_V].astype(BF16)


def _inproj(h, nb, s, win, gq, wuq, gkv, wukv, tabs_da, tabs_mla, *, tm=256):
    m, d = h.shape
    tm = min(tm, s)
    assert s % tm == 0 and m == nb * s
    spb = s // tm
    tab_spec = pl.BlockSpec((tm, LANES), lambda i: (i % spb, 0))

    def head_spec(*tail):
        n = len(tail)
        return pl.BlockSpec((1, DA_HEADS) + tail, lambda i: (i // spb, 0) + (0,) * (n - 2) + (i % spb, 0))

    def head_shape(*tail):
        return jax.ShapeDtypeStruct((nb, DA_HEADS) + tail, BF16)

    vmem = (2 * tm * d * 4 + (win.size + wuq.size + wukv.size) * 2 + 2 * 6 * tm * LANES * 4
            + 2 * tm * 8 * (256 + 128 + 128 + 256 + 256 + 128) * 2 + 8 * tm * 2048 * 4) / MIB + 6
    return pl.pallas_call(
        _inproj_kernel,
        grid=(m // tm,),
        in_specs=[
            pl.BlockSpec((tm, d), lambda i: (i, 0)),
            _resident(win.shape), _resident(gq.shape), _resident(wuq.shape),
            _resident(gkv.shape), _resident(wukv.shape),
            tab_spec, tab_spec, tab_spec, tab_spec, tab_spec, tab_spec,
        ],
        out_specs=[
            head_spec(2, tm, LANES), head_spec(tm, LANES), head_spec(tm, DA_V_DIM),
            head_spec(tm, MLA_QK_PAD), head_spec(tm, MLA_QK_PAD), head_spec(tm, MLA_V),
        ],
        out_shape=[
            head_shape(2, s, LANES), head_shape(s, LANES), head_shape(s, DA_V_DIM),
            head_shape(s, MLA_QK_PAD), head_shape(s, MLA_QK_PAD), head_shape(s, MLA_V),
        ],
        compiler_params=_cparams(1, vmem),
        name="inproj",
    )(h, win, gq, wuq, gkv, wukv, *tabs_da, *tabs_mla)


def _dot_nt(a, b):
    return lax.dot_general(a, b, (((1,), (1,)), ((), ())), preferred_element_type=F32)


def _flash_rows(q, k_ref, v_ref, kmeta, vmeta, m_scr, l_scr, acc_scr, *, tk, exp_scale):
    s_len = k_ref.shape[2]
    s = _dot_nt(q, kmeta)
    col = lax.broadcasted_iota(jnp.int32, s.shape, 1)
    s = jnp.where(col < N_META, s, NEG_BIG)
    m0 = jnp.max(s, axis=1, keepdims=True)
    p = jnp.exp2((s - m0) * exp_scale)
    m_scr[...] = m0
    l_scr[...] = jnp.sum(p, axis=1, keepdims=True)
    acc_scr[...] = jnp.dot(p.astype(BF16), vmeta, preferred_element_type=F32)

    def body(c, carry):
        off = pl.multiple_of(c * tk, tk)
        kc = k_ref[0, 0, pl.ds(off, tk), :]
        vc = v_ref[0, 0, pl.ds(off, tk), :]
        s = _dot_nt(q, kc)
        m_prev = m_scr[...]
        m_new = jnp.maximum(m_prev, jnp.max(s, axis=1, keepdims=True))
        alpha = jnp.exp2((m_prev - m_new) * exp_scale)
        p = jnp.exp2((s - m_new) * exp_scale)
        l_scr[...] = alpha * l_scr[...] + jnp.sum(p, axis=1, keepdims=True)
        acc_scr[...] = alpha * acc_scr[...] + jnp.dot(p.astype(BF16), vc, preferred_element_type=F32)
        m_scr[...] = m_new
        return carry

    lax.fori_loop(0, s_len // tk, body, 0)


def _da_attn_kernel(q_ref, k_ref, v_ref, kmeta_ref, vmeta_ref, lq1_ref, lk1_ref, lq2_ref, lk2_ref,
                    g_ref, o_ref, m_scr, l_scr, acc_scr, *, tk):
    tq = q_ref.shape[3]
    q = q_ref[0, 0].reshape(2 * tq, LANES)
    _flash_rows(q, k_ref, v_ref, kmeta_ref[0, 0], vmeta_ref[0, 0], m_scr, l_scr, acc_scr,
                tk=tk, exp_scale=LOG2E)
    a = acc_scr[...] * (1.0 / l_scr[...])
    lam = (jnp.exp(jnp.sum(lq1_ref[...] * lk1_ref[...], axis=1, keepdims=True))
           - jnp.exp(jnp.sum(lq2_ref[...] * lk2_ref[...], axis=1, keepdims=True)) + LAMBDA_INIT)
    o = a[:tq] - lam * a[tq:]
    o_ref[0] = (_rms_norm(o, g_ref[...]) * (1.0 - LAMBDA_INIT)).astype(BF16)


def _mla_attn_kernel(q_ref, k_ref, v_ref, kmeta_ref, vmeta_ref, o_ref, m_scr, l_scr, acc_scr, *, tk):
    scale = (MLA_NOPE + MLA_ROPE) ** -0.5
    _flash_rows(q_ref[0, 0], k_ref, v_ref, kmeta_ref[0, 0], vmeta_ref[0, 0], m_scr, l_scr, acc_scr,
                tk=tk, exp_scale=scale * LOG2E)
    o_ref[0] = (acc_scr[...] * (1.0 / l_scr[...])).astype(BF16)


def _attn_vmem(rows, tk, s, dk, dv):
    return (2 * s * (dk + dv) * 2 + 4 * rows * dk * 2 + rows * (dv + 2 * LANES) * 4
            + 3 * rows * tk * 4 + 2 * rows * dv * 2) / MIB + 8


def _da_attn(qda, kda, vda, kmeta, vmeta, lam_vecs, g, *, tq=256, tk=512):
    nb, nh, _, s, _ = qda.shape
    tq, tk = min(tq, s), min(tk, s)
    assert s % tq == 0 and s % tk == 0
    rows = 2 * tq
    kv_spec = pl.BlockSpec((1, 1, s, LANES), lambda b, h, i: (b, h, 0, 0))
    meta_spec = pl.BlockSpec((1, 1, META_PAD, LANES), lambda b, h, i: (0, h, 0, 0))
    vec_spec = pl.BlockSpec((1, DA_HEAD_DIM), lambda b, h, i: (0, 0))
    return pl.pallas_call(
        functools.partial(_da_attn_kernel, tk=tk),
        grid=(nb, nh, s // tq),
        in_specs=[
            pl.BlockSpec((1, 1, 2, tq, LANES), lambda b, h, i: (b, h, 0, i, 0)),
            kv_spec, kv_spec, meta_spec, meta_spec,
            vec_spec, vec_spec, vec_spec, vec_spec,
            pl.BlockSpec((1, DA_V_DIM), lambda b, h, i: (0, 0)),
        ],
        out_specs=pl.BlockSpec((1, tq, DA_V_DIM), lambda b, h, i: (b, i, h)),
        out_shape=jax.ShapeDtypeStruct((nb, s, DA_WIDTH), BF16),
        scratch_shapes=[pltpu.VMEM((rows, 1), F32), pltpu.VMEM((rows, 1), F32),
                        pltpu.VMEM((rows, DA_V_DIM), F32)],
        compiler_params=_cparams(3, _attn_vmem(rows, tk, s, LANES, DA_V_DIM)),
        name="da_attn",
    )(qda, kda, vda, kmeta, vmeta, *lam_vecs, g)


def _mla_attn(qm, km, vm, kmeta, vmeta, *, tq=512, tk=512):
    nb, nh, s, _ = qm.shape
    tq, tk = min(tq, s), min(tk, s)
    assert s % tq == 0 and s % tk == 0
    return pl.pallas_call(
        functools.partial(_mla_attn_kernel, tk=tk),
        grid=(nb, nh, s // tq),
        in_specs=[
            pl.BlockSpec((1, 1, tq, MLA_QK_PAD), lambda b, h, i: (b, h, i, 0)),
            pl.BlockSpec((1, 1, s, MLA_QK_PAD), lambda b, h, i: (b, h, 0, 0)),
            pl.BlockSpec((1, 1, s, MLA_V), lambda b, h, i: (b, h, 0, 0)),
            pl.BlockSpec((1, 1, META_PAD, MLA_QK_PAD), lambda b, h, i: (0, h, 0, 0)),
            pl.BlockSpec((1, 1, META_PAD, MLA_V), lambda b, h, i: (0, h, 0, 0)),
        ],
        out_specs=pl.BlockSpec((1, tq, MLA_V), lambda b, h, i: (b, i, h)),
        out_shape=jax.ShapeDtypeStruct((nb, s, MLA_WIDTH), BF16),
        scratch_shapes=[pltpu.VMEM((tq, 1), F32), pltpu.VMEM((tq, 1), F32),
                        pltpu.VMEM((tq, MLA_V), F32)],
        compiler_params=_cparams(3, _attn_vmem(tq, tk, s, MLA_QK_PAD, MLA_V)),
        name="mla_attn",
    )(qm, km, vm, kmeta, vmeta)


def _outproj_ln_kernel(h_ref, oa_ref, om_ref, wo_ref, g_ref, b_ref, o_ref):
    mix = (jnp.dot(oa_ref[...], wo_ref[0:DA_WIDTH, :], preferred_element_type=F32)
           + jnp.dot(om_ref[...], wo_ref[DA_WIDTH:DA_WIDTH + MLA_WIDTH, :], preferred_element_type=F32))
    o_ref[...] = _layer_norm(ALPHA * h_ref[...] + mix, g_ref[...], b_ref[...])


def _outproj_ln(h, oa, om, wo, g, b, *, tm=512):
    m, d = h.shape
    tm = min(tm, m)
    assert m % tm == 0
    vmem = (2 * 2 * tm * d * 4 + 2 * 2 * tm * DA_WIDTH * 2 + wo.size * 2 + 3 * tm * d * 4) / MIB + 6
    return pl.pallas_call(
        _outproj_ln_kernel,
        grid=(m // tm,),
        in_specs=[
            pl.BlockSpec((tm, d), lambda i: (i, 0)),
            pl.BlockSpec((tm, DA_WIDTH), lambda i: (i, 0)),
            pl.BlockSpec((tm, MLA_WIDTH), lambda i: (i, 0)),
            _resident(wo.shape), _resident(g.shape), _resident(b.shape),
        ],
        out_specs=pl.BlockSpec((tm, d), lambda i: (i, 0)),
        out_shape=jax.ShapeDtypeStruct((m, d), F32),
        compiler_params=_cparams(1, vmem),
        name="outproj_ln",
    )(h, oa, om, wo, g, b)


def _rope_lane_tables(n_pos, rot_dim, period):
    half = rot_dim // 2
    inv = 1.0 / (ROPE_THETA ** (jnp.arange(0, rot_dim, 2, dtype=F32) / rot_dim))
    ang = jnp.arange(n_pos, dtype=F32)[:, None] * inv[None, :]
    cos, sin = jnp.cos(ang), jnp.sin(ang)
    within = jnp.arange(LANES) % period
    idx = within % half
    c = jnp.where(within < rot_dim, cos[:, idx], 1.0)
    sp = jnp.where((within >= half) & (within < rot_dim), sin[:, idx], 0.0)
    sm = jnp.where(within < half, -sin[:, idx], 0.0)
    return c, sp, sm


def _prep_weights(p):
    w = {}
    for name in ("ffn1_w_gate", "ffn1_w_up", "ffn1_w_down", "ffn2_w_gate", "ffn2_w_up", "ffn2_w_down", "w_o", "w_ukv"):
        w[name] = p[name][0].astype(BF16)
    w_in = p["w_in"][0]
    w["w_in"] = jnp.pad(w_in, ((0, 0), (0, LANES - MLA_ROPE))).astype(BF16)
    w_uq = p["w_uq"][0].reshape(MLA_Q_RANK, MLA_HEADS, MLA_NOPE + MLA_ROPE)
    w_uq = jnp.pad(w_uq, ((0, 0), (0, 0), (0, MLA_QK_PAD - MLA_NOPE - MLA_ROPE)))
    w["w_uq"] = w_uq.reshape(MLA_Q_RANK, MLA_HEADS * MLA_QK_PAD).astype(BF16)
    for name in ("ln1_g", "ln1_b", "ln2_g", "ln2_b", "ln3_g", "ln3_b", "da_subln_g", "mla_q_norm_g", "mla_kv_norm_g",
                 "da_lambda_q1", "da_lambda_k1", "da_lambda_q2", "da_lambda_k2"):
        w[name] = p[name].astype(F32)
    return w


def _pad_meta(x):
    return jnp.pad(x, ((0, 0), (0, 0), (0, META_PAD - N_META), (0, 0)))


def _mixer_inputs(h, nb, s, w, tabs_da, tabs_mla):
    return _inproj(h, nb, s, w["w_in"], w["mla_q_norm_g"], w["w_uq"], w["mla_kv_norm_g"], w["w_ukv"],
                   tabs_da, tabs_mla)


def _trunk(x, meta_kv, w, tabs_da, tabs_mla):
    nb, s, d = x.shape
    h1 = _ffn_ln(x.reshape(nb * s, d), w["ffn1_w_gate"], w["ffn1_w_up"], w["ffn1_w_down"], w["ln1_g"], w["ln1_b"])
    qda, kda, vda, qm, km, vm = _mixer_inputs(h1, nb, s, w, tabs_da, tabs_mla)
    kda_m, vda_m, km_m, vm_m = meta_kv
    lam_vecs = (w["da_lambda_q1"], w["da_lambda_k1"], w["da_lambda_q2"], w["da_lambda_k2"])
    oa = _da_attn(qda, kda, vda, kda_m, vda_m, lam_vecs, w["da_subln_g"])
    om = _mla_attn(qm, km, vm, km_m, vm_m)
    h2 = _outproj_ln(h1, oa.reshape(nb * s, DA_WIDTH), om.reshape(nb * s, MLA_WIDTH), w["w_o"], w["ln2_g"], w["ln2_b"])
    y = _ffn_ln(h2, w["ffn2_w_gate"], w["ffn2_w_up"], w["ffn2_w_down"], w["ln3_g"], w["ln3_b"])
    return y.reshape(nb, s, d)


def kernel(x_prompt, x_sample, meta_tokens, ffn1_w_gate, ffn1_w_up, ffn1_w_down, ln1_g, ln1_b, w_in, da_lambda_q1, da_lambda_k1, da_lambda_q2, da_lambda_k2, da_subln_g, mla_q_norm_g, w_uq, mla_kv_norm_g, w_ukv, w_o, ln2_g, ln2_b, ffn2_w_gate, ffn2_w_up, ffn2_w_down, ln3_g, ln3_b):
    p = dict(ffn1_w_gate=ffn1_w_gate, ffn1_w_up=ffn1_w_up, ffn1_w_down=ffn1_w_down, ln1_g=ln1_g, ln1_b=ln1_b,
             w_in=w_in, da_lambda_q1=da_lambda_q1, da_lambda_k1=da_lambda_k1, da_lambda_q2=da_lambda_q2,
             da_lambda_k2=da_lambda_k2, da_subln_g=da_subln_g, mla_q_norm_g=mla_q_norm_g, w_uq=w_uq,
             mla_kv_norm_g=mla_kv_norm_g, w_ukv=w_ukv, w_o=w_o, ln2_g=ln2_g, ln2_b=ln2_b,
             ffn2_w_gate=ffn2_w_gate, ffn2_w_up=ffn2_w_up, ffn2_w_down=ffn2_w_down, ln3_g=ln3_g, ln3_b=ln3_b)
    w = _prep_weights(p)
    s_max = max(x_prompt.shape[1], x_sample.shape[1])
    tabs_da = _rope_lane_tables(N_META + s_max, DA_ROT, DA_HEAD_DIM)
    tabs_mla = _rope_lane_tables(N_META + s_max, MLA_ROPE, LANES)

    hm = _ffn_ln(meta_tokens.astype(F32), w["ffn1_w_gate"], w["ffn1_w_up"], w["ffn1_w_down"], w["ln1_g"], w["ln1_b"])
    _, kda_m, vda_m, _, km_m, vm_m = _mixer_inputs(
        hm, 1, N_META, w, [t[:N_META] for t in tabs_da], [t[:N_META] for t in tabs_mla])
    meta_kv = tuple(_pad_meta(t) for t in (kda_m, vda_m, km_m, vm_m))

    seq_da = [t[N_META:] for t in tabs_da]
    seq_mla = [t[N_META:] for t in tabs_mla]
    y_prompt = _trunk(x_prompt, meta_kv, w, seq_da, seq_mla)
    y_sample = _trunk(x_sample, meta_kv, w, seq_da, seq_mla)
    return (y_prompt, y_sample)
```

```python
import functools
import math

import jax
import jax.numpy as jnp
from jax import lax
from jax.experimental import pallas as pl
from jax.experimental.pallas import tpu as pltpu

N_META = 16
ROPE_THETA = 500000.0
DA_HEADS = 8
DA_HEAD_DIM = 64
DA_V_DIM = 2 * DA_HEAD_DIM
DA_ROT = DA_HEAD_DIM // 4
DA_QK_COLS = DA_HEADS * 2 * DA_HEAD_DIM
DA_WIDTH = DA_HEADS * DA_V_DIM
MLA_HEADS = 8
MLA_Q_RANK = 512
MLA_KV_RANK = 256
MLA_NOPE = 128
MLA_ROPE = 64
MLA_V = 128
MLA_WIDTH = MLA_HEADS * MLA_V
MLA_QK_PAD = 256
DEPTH = 1
ALPHA = (2 * DEPTH) ** 0.25
LN_EPS = 1e-5
RMS_EPS = 1e-6
LAMBDA_INIT = 0.8 - 0.6 * math.exp(-0.3 * 0)
LOG2E = 1.4426950408889634

LANES = 128
SUBLANES = 8
META_PAD = 128
NEG_BIG = -1e30

PROJ_ROWS = 256
MLA_Q_TILE = 512
KV_CHUNK = 512
V_EXTRA_ROWS = 16
V_ROWS = DA_V_DIM + V_EXTRA_ROWS

F32 = jnp.float32
BF16 = jnp.bfloat16
MIB = 1024 * 1024


def _cparams(n_axes, vmem_mib):
    return pltpu.CompilerParams(
        dimension_semantics=("arbitrary",) * n_axes,
        vmem_limit_bytes=int(vmem_mib * MIB),
    )


def _resident(shape):
    n = len(shape)
    return pl.BlockSpec(shape, lambda *_: (0,) * n, pipeline_mode=pl.Buffered(1))


def _layer_norm(y, g, b):
    mu = jnp.mean(y, axis=-1, keepdims=True)
    d = y - mu
    var = jnp.mean(d * d, axis=-1, keepdims=True)
    return d * lax.rsqrt(var + LN_EPS) * g + b


def _rms_norm(x, g):
    return x * lax.rsqrt(jnp.mean(x * x, axis=-1, keepdims=True) + RMS_EPS) * g


def _ffn_ln_kernel(x_ref, wg_ref, wu_ref, wd_ref, g_ref, b_ref, o_ref, xb_ref, acc_ref):
    j = pl.program_id(1)

    @pl.when(j == 0)
    def _():
        xb_ref[...] = x_ref[...].astype(BF16)
        acc_ref[...] = jnp.zeros_like(acc_ref)

    xb = xb_ref[...]
    gate = jnp.dot(xb, wg_ref[...], preferred_element_type=F32)
    up = jnp.dot(xb, wu_ref[...], preferred_element_type=F32)
    hid = (gate * (1.0 / (1.0 + jnp.exp(-gate)))) * up
    acc_ref[...] += jnp.dot(hid.astype(BF16), wd_ref[...], preferred_element_type=F32)

    @pl.when(j == pl.num_programs(1) - 1)
    def _():
        y = ALPHA * x_ref[...] + 0.5 * acc_ref[...]
        o_ref[...] = _layer_norm(y, g_ref[...], b_ref[...])


def _ffn_ln(x, wg, wu, wd, g, b, *, tm=512, tf=512):
    m, d = x.shape
    f = wg.shape[1]
    tm = min(tm, m)
    tf = min(tf, f)
    assert m % tm == 0 and f % tf == 0
    vmem = (2 * 2 * tm * d * 4 + 2 * 3 * d * tf * 2 + tm * d * 6 + 4 * tm * tf * 4) / MIB + 6
    return pl.pallas_call(
        _ffn_ln_kernel,
        grid=(m // tm, f // tf),
        in_specs=[
            pl.BlockSpec((tm, d), lambda i, j: (i, 0)),
            pl.BlockSpec((d, tf), lambda i, j: (0, j)),
            pl.BlockSpec((d, tf), lambda i, j: (0, j)),
            pl.BlockSpec((tf, d), lambda i, j: (j, 0)),
            pl.BlockSpec((1, d), lambda i, j: (0, 0)),
            pl.BlockSpec((1, d), lambda i, j: (0, 0)),
        ],
        out_specs=pl.BlockSpec((tm, d), lambda i, j: (i, 0)),
        out_shape=jax.ShapeDtypeStruct((m, d), F32),
        scratch_shapes=[pltpu.VMEM((tm, d), BF16), pltpu.VMEM((tm, d), F32)],
        compiler_params=_cparams(2, vmem),
        name="ffn_ln",
    )(x, wg, wu, wd, g, b)


def _rope(x, c, sp, sm, half):
    return x * c + pltpu.roll(x, half, 1) * sp + pltpu.roll(x, LANES - half, 1) * sm


def _inproj_kernel(h_ref, win_ref, gq_ref, wuq_ref, gkv_ref, wukv_ref,
                   cda_ref, spda_ref, smda_ref, cm_ref, spm_ref, smm_ref,
                   qdat_ref, kda_ref, vdat_ref, qmt_ref, km_ref, vmt_ref):
    tm = h_ref.shape[0]
    hb = h_ref[...].astype(BF16)

    def proj(lo, hi):
        return jnp.dot(hb, win_ref[:, lo:hi], preferred_element_type=F32)

    cda, spda, smda = cda_ref[...], spda_ref[...], smda_ref[...]
    cm, spm, smm = cm_ref[...], spm_ref[...], smm_ref[...]
    da_half = DA_ROT // 2
    mla_half = MLA_ROPE // 2
    map1_dims = lax.broadcasted_iota(jnp.int32, (LANES, tm), 0) < DA_HEAD_DIM
    da_scale = DA_HEAD_DIM ** -0.5 * LOG2E
    mla_scale = (MLA_NOPE + MLA_ROPE) ** -0.5 * LOG2E
    sum_rows = (lax.broadcasted_iota(jnp.int32, (V_EXTRA_ROWS, tm), 0) == 0).astype(BF16)

    c0 = 0
    zq = proj(c0, c0 + DA_QK_COLS)
    for h in range(DA_HEADS):
        rt = (_rope(zq[:, h * LANES:(h + 1) * LANES], cda, spda, smda, da_half) * da_scale).T
        qdat_ref[0, h, 0, :, 0:tm] = jnp.where(map1_dims, rt, 0.0).astype(BF16)
        qdat_ref[0, h, 0, :, tm:2 * tm] = jnp.where(map1_dims, 0.0, rt).astype(BF16)
    c0 += DA_QK_COLS
    zk = proj(c0, c0 + DA_QK_COLS)
    for h in range(DA_HEADS):
        kda_ref[0, h] = _rope(zk[:, h * LANES:(h + 1) * LANES], cda, spda, smda, da_half).astype(BF16)
    c0 += DA_QK_COLS
    zv = proj(c0, c0 + DA_WIDTH)
    for h in range(DA_HEADS):
        vdat_ref[0, h, 0, 0:DA_V_DIM, :] = zv[:, h * DA_V_DIM:(h + 1) * DA_V_DIM].T.astype(BF16)
        vdat_ref[0, h, 0, DA_V_DIM:V_ROWS, :] = sum_rows
    c0 += DA_WIDTH

    cq = _rms_norm(proj(c0, c0 + MLA_Q_RANK), gq_ref[...]).astype(BF16)
    c0 += MLA_Q_RANK
    qm = jnp.dot(cq, wuq_ref[...], preferred_element_type=F32) * mla_scale
    for h in range(MLA_HEADS):
        base = h * MLA_QK_PAD
        qmt_ref[0, h, 0, 0:MLA_NOPE, :] = qm[:, base:base + MLA_NOPE].T.astype(BF16)
        qmt_ref[0, h, 0, MLA_NOPE:MLA_QK_PAD, :] = _rope(
            qm[:, base + MLA_NOPE:base + MLA_QK_PAD], cm, spm, smm, mla_half).T.astype(BF16)

    ckv = _rms_norm(proj(c0, c0 + MLA_KV_RANK), gkv_ref[...]).astype(BF16)
    c0 += MLA_KV_RANK
    kv = jnp.dot(ckv, wukv_ref[...], preferred_element_type=F32)
    kr = _rope(proj(c0, c0 + LANES), cm, spm, smm, mla_half).astype(BF16)
    for h in range(MLA_HEADS):
        base = h * (MLA_NOPE + MLA_V)
        km_ref[0, h, :, 0:MLA_NOPE] = kv[:, base:base + MLA_NOPE].astype(BF16)
        km_ref[0, h, :, MLA_NOPE:MLA_QK_PAD] = kr
        vmt_ref[0, h, 0, 0:MLA_V, :] = kv[:, base + MLA_NOPE:base + MLA_NOPE + MLA_V].T.astype(BF16)
        vmt_ref[0, h, 0, MLA_V:V_ROWS, :] = sum_rows


def _inproj(h, nb, s, win, gq, wuq, gkv, wukv, tabs_da, tabs_mla):
    m, d = h.shape
    tm = PROJ_ROWS
    ck = min(KV_CHUNK, s)
    tqm = min(MLA_Q_TILE, s)
    assert m == nb * s and s % tm == 0 and ck % tm == 0 and tqm % tm == 0 and s % ck == 0 and s % tqm == 0
    spb = s // tm
    tab_spec = pl.BlockSpec((tm, LANES), lambda i: (i % spb, 0))

    def rows_spec(width):
        return pl.BlockSpec((1, DA_HEADS, tm, width), lambda i: (i // spb, 0, i % spb, 0))

    def cols_spec(rows, tile):
        per = tile // tm
        return pl.BlockSpec((1, DA_HEADS, 1, rows, tm),
                            lambda i: (i // spb, 0, (i % spb) // per, 0, (i % spb) % per))

    def shape(*tail):
        return jax.ShapeDtypeStruct((nb, DA_HEADS) + tail, BF16)

    vmem = (2 * tm * d * 4 + (win.size + wuq.size + wukv.size) * 2 + 2 * 6 * tm * LANES * 4
            + 2 * tm * 8 * (256 + 128 + 128 + 256 + 256 + 128) * 2 + 8 * tm * 2048 * 4) / MIB + 6
    return pl.pallas_call(
        _inproj_kernel,
        grid=(m // tm,),
        in_specs=[
            pl.BlockSpec((tm, d), lambda i: (i, 0)),
            _resident(win.shape), _resident(gq.shape), _resident(wuq.shape),
            _resident(gkv.shape), _resident(wukv.shape),
            tab_spec, tab_spec, tab_spec, tab_spec, tab_spec, tab_spec,
        ],
        out_specs=[
            pl.BlockSpec((1, DA_HEADS, 1, LANES, 2 * tm), lambda i: (i // spb, 0, i % spb, 0, 0)),
            rows_spec(LANES), cols_spec(V_ROWS, ck),
            cols_spec(MLA_QK_PAD, tqm), rows_spec(MLA_QK_PAD), cols_spec(V_ROWS, ck),
        ],
        out_shape=[
            shape(s // tm, LANES, 2 * tm), shape(s, LANES), shape(s // ck, V_ROWS, ck),
            shape(s // tqm, MLA_QK_PAD, tqm), shape(s, MLA_QK_PAD), shape(s // ck, V_ROWS, ck),
        ],
        compiler_params=_cparams(1, vmem),
        name="inproj",
    )(h, win, gq, wuq, gkv, wukv, *tabs_da, *tabs_mla)


def _flash_cols(qt, k_ref, vt_ref, kmeta, vmeta_t, s_scrs, cmax_scrs, m_scr, acc_scr):
    n_chunks, tk = vt_ref.shape[2], vt_ref.shape[4]
    r = qt.shape[1]

    def bcast(x):
        return jnp.broadcast_to(x, (SUBLANES, r))

    def scores(c, slot):
        off = c * tk if isinstance(c, int) else pl.multiple_of(c * tk, tk)
        s = jnp.dot(k_ref[0, 0, pl.ds(off, tk), :], qt, preferred_element_type=F32)
        s_scrs[slot][...] = s
        cmax_scrs[slot][...] = bcast(jnp.max(s, axis=0, keepdims=True))

    def fold(s, cmax, vt, first=False):
        m_new = cmax if first else jnp.maximum(m_scr[0:1, :], cmax)
        p = jnp.exp2(s - m_new)
        pv = jnp.dot(vt, p.astype(BF16), preferred_element_type=F32)
        if first:
            acc_scr[...] = pv
        else:
            acc_scr[...] = jnp.exp2(m_scr[0:1, :] - m_new) * acc_scr[...] + pv
        m_scr[...] = bcast(m_new)

    def fold_slot(c, slot):
        fold(s_scrs[slot][...], cmax_scrs[slot][0:1, :], vt_ref[0, 0, c])

    scores(0, 0)
    s = jnp.dot(kmeta, qt, preferred_element_type=F32)
    s = jnp.where(lax.broadcasted_iota(jnp.int32, s.shape, 0) < N_META, s, NEG_BIG)
    fold(s, jnp.max(s, axis=0, keepdims=True), vmeta_t, first=True)

    n_pairs = (n_chunks - 1) // 2

    def pair(j, carry):
        scores(2 * j + 1, 1)
        fold_slot(2 * j, 0)
        scores(2 * j + 2, 0)
        fold_slot(2 * j + 1, 1)
        return carry

    lax.fori_loop(0, n_pairs, pair, 0)
    done = 2 * n_pairs
    if n_chunks - done == 2:
        scores(done + 1, 1)
        fold_slot(done, 0)
        fold_slot(done + 1, 1)
    else:
        fold_slot(done, 0)


def _normalised(acc_scr, dv):
    return acc_scr[0:dv, :] * (1.0 / acc_scr[dv:dv + 1, :])


def _da_attn_kernel(qt_ref, k_ref, vt_ref, kmeta_ref, vmeta_ref, lq1_ref, lk1_ref, lq2_ref, lk2_ref,
                    g_ref, o_ref, s0_scr, s1_scr, c0_scr, c1_scr, m_scr, acc_scr):
    tq = o_ref.shape[1]
    _flash_cols(qt_ref[0, 0, 0], k_ref, vt_ref, kmeta_ref[0, 0], vmeta_ref[0, 0],
                (s0_scr, s1_scr), (c0_scr, c1_scr), m_scr, acc_scr)
    at = _normalised(acc_scr, DA_V_DIM)
    lam = (jnp.exp(jnp.sum(lq1_ref[...] * lk1_ref[...], axis=1, keepdims=True))
           - jnp.exp(jnp.sum(lq2_ref[...] * lk2_ref[...], axis=1, keepdims=True)) + LAMBDA_INIT)
    o = (at[:, :tq] - lam * at[:, tq:]).T
    o_ref[0] = (_rms_norm(o, g_ref[...]) * (1.0 - LAMBDA_INIT)).astype(BF16)


def _mla_attn_kernel(qt_ref, k_ref, vt_ref, kmeta_ref, vmeta_ref, o_ref,
                     s0_scr, s1_scr, c0_scr, c1_scr, m_scr, acc_scr):
    _flash_cols(qt_ref[0, 0, 0], k_ref, vt_ref, kmeta_ref[0, 0], vmeta_ref[0, 0],
                (s0_scr, s1_scr), (c0_scr, c1_scr), m_scr, acc_scr)
    o_ref[0] = _normalised(acc_scr, MLA_V).T.astype(BF16)


def _attn_scratch(cols, tk):
    stat = pltpu.VMEM((SUBLANES, cols), F32)
    return [pltpu.VMEM((tk, cols), F32), pltpu.VMEM((tk, cols), F32), stat, stat, stat,
            pltpu.VMEM((V_ROWS, cols), F32)]


def _attn_vmem(cols, tk, s, dk, dv):
    return (2 * s * (dk + dv) * 2 + 2 * dk * cols * 2 + 2 * tk * cols * 4 + dv * cols * 4
            + 4 * tk * cols * 4 + 2 * cols * dv * 2) / MIB + 8


def _da_attn(qdat, kda, vdat, kmeta, vmeta_t, lam_vecs, g):
    nb, nh, nq, _, cols = qdat.shape
    tq = cols // 2
    s = kda.shape[2]
    nc, tk = vdat.shape[2], vdat.shape[4]
    vec_spec = pl.BlockSpec((1, DA_HEAD_DIM), lambda b, h, i: (0, 0))
    meta_spec = pl.BlockSpec((1, 1, META_PAD, LANES), lambda b, h, i: (0, h, 0, 0))
    vmeta_spec = pl.BlockSpec((1, 1, V_ROWS, META_PAD), lambda b, h, i: (0, h, 0, 0))
    return pl.pallas_call(
        _da_attn_kernel,
        grid=(nb, nh, nq),
        in_specs=[
            pl.BlockSpec((1, 1, 1, LANES, cols), lambda b, h, i: (b, h, i, 0, 0)),
            pl.BlockSpec((1, 1, s, LANES), lambda b, h, i: (b, h, 0, 0)),
            pl.BlockSpec((1, 1, nc, V_ROWS, tk), lambda b, h, i: (b, h, 0, 0, 0)),
            meta_spec, vmeta_spec,
            vec_spec, vec_spec, vec_spec, vec_spec,
            pl.BlockSpec((1, DA_V_DIM), lambda b, h, i: (0, 0)),
        ],
        out_specs=pl.BlockSpec((1, tq, DA_V_DIM), lambda b, h, i: (b, i, h)),
        out_shape=jax.ShapeDtypeStruct((nb, s, DA_WIDTH), BF16),
        scratch_shapes=_attn_scratch(cols, tk),
        compiler_params=_cparams(3, _attn_vmem(cols, tk, s, LANES, DA_V_DIM)),
        name="da_attn",
    )(qdat, kda, vdat, kmeta, vmeta_t, *lam_vecs, g)


def _mla_attn(qmt, km, vmt, kmeta, vmeta_t):
    nb, nh, nq, _, tq = qmt.shape
    s = km.shape[2]
    nc, tk = vmt.shape[2], vmt.shape[4]
    return pl.pallas_call(
        _mla_attn_kernel,
        grid=(nb, nh, nq),
        in_specs=[
            pl.BlockSpec((1, 1, 1, MLA_QK_PAD, tq), lambda b, h, i: (b, h, i, 0, 0)),
            pl.BlockSpec((1, 1, s, MLA_QK_PAD), lambda b, h, i: (b, h, 0, 0)),
            pl.BlockSpec((1, 1, nc, V_ROWS, tk), lambda b, h, i: (b, h, 0, 0, 0)),
            pl.BlockSpec((1, 1, META_PAD, MLA_QK_PAD), lambda b, h, i: (0, h, 0, 0)),
            pl.BlockSpec((1, 1, V_ROWS, META_PAD), lambda b, h, i: (0, h, 0, 0)),
        ],
        out_specs=pl.BlockSpec((1, tq, MLA_V), lambda b, h, i: (b, i, h)),
        out_shape=jax.ShapeDtypeStruct((nb, s, MLA_WIDTH), BF16),
        scratch_shapes=_attn_scratch(tq, tk),
        compiler_params=_cparams(3, _attn_vmem(tq, tk, s, MLA_QK_PAD, MLA_V)),
        name="mla_attn",
    )(qmt, km, vmt, kmeta, vmeta_t)


def _outproj_ln_kernel(h_ref, oa_ref, om_ref, wo_ref, g_ref, b_ref, o_ref):
    mix = (jnp.dot(oa_ref[...], wo_ref[0:DA_WIDTH, :], preferred_element_type=F32)
           + jnp.dot(om_ref[...], wo_ref[DA_WIDTH:DA_WIDTH + MLA_WIDTH, :], preferred_element_type=F32))
    o_ref[...] = _layer_norm(ALPHA * h_ref[...] + mix, g_ref[...], b_ref[...])


def _outproj_ln(h, oa, om, wo, g, b, *, tm=512):
    m, d = h.shape
    tm = min(tm, m)
    assert m % tm == 0
    vmem = (2 * 2 * tm * d * 4 + 2 * 2 * tm * DA_WIDTH * 2 + wo.size * 2 + 3 * tm * d * 4) / MIB + 6
    return pl.pallas_call(
        _outproj_ln_kernel,
        grid=(m // tm,),
        in_specs=[
            pl.BlockSpec((tm, d), lambda i: (i, 0)),
            pl.BlockSpec((tm, DA_WIDTH), lambda i: (i, 0)),
            pl.BlockSpec((tm, MLA_WIDTH), lambda i: (i, 0)),
            _resident(wo.shape), _resident(g.shape), _resident(b.shape),
        ],
        out_specs=pl.BlockSpec((tm, d), lambda i: (i, 0)),
        out_shape=jax.ShapeDtypeStruct((m, d), F32),
        compiler_params=_cparams(1, vmem),
        name="outproj_ln",
    )(h, oa, om, wo, g, b)


def _rope_lane_tables(n_pos, rot_dim, period):
    half = rot_dim // 2
    inv = 1.0 / (ROPE_THETA ** (jnp.arange(0, rot_dim, 2, dtype=F32) / rot_dim))
    ang = jnp.arange(n_pos, dtype=F32)[:, None] * inv[None, :]
    cos, sin = jnp.cos(ang), jnp.sin(ang)
    within = jnp.arange(LANES) % period
    idx = within % half
    c = jnp.where(within < rot_dim, cos[:, idx], 1.0)
    sp = jnp.where((within >= half) & (within < rot_dim), sin[:, idx], 0.0)
    sm = jnp.where(within < half, -sin[:, idx], 0.0)
    return c, sp, sm


def _prep_weights(p):
    w = {}
    for name in ("ffn1_w_gate", "ffn1_w_up", "ffn1_w_down", "ffn2_w_gate", "ffn2_w_up", "ffn2_w_down", "w_o", "w_ukv"):
        w[name] = p[name][0].astype(BF16)
    w_in = p["w_in"][0]
    w["w_in"] = jnp.pad(w_in, ((0, 0), (0, LANES - MLA_ROPE))).astype(BF16)
    w_uq = p["w_uq"][0].reshape(MLA_Q_RANK, MLA_HEADS, MLA_NOPE + MLA_ROPE)
    w_uq = jnp.pad(w_uq, ((0, 0), (0, 0), (0, MLA_QK_PAD - MLA_NOPE - MLA_ROPE)))
    w["w_uq"] = w_uq.reshape(MLA_Q_RANK, MLA_HEADS * MLA_QK_PAD).astype(BF16)
    for name in ("ln1_g", "ln1_b", "ln2_g", "ln2_b", "ln3_g", "ln3_b", "da_subln_g", "mla_q_norm_g", "mla_kv_norm_g",
                 "da_lambda_q1", "da_lambda_k1", "da_lambda_q2", "da_lambda_k2"):
        w[name] = p[name].astype(F32)
    return w


def _mixer_inputs(h, nb, s, w, tabs_da, tabs_mla):
    return _inproj(h, nb, s, w["w_in"], w["mla_q_norm_g"], w["w_uq"], w["mla_kv_norm_g"], w["w_ukv"],
                   tabs_da, tabs_mla)


def _trunk(x, meta_kv, w, tabs_da, tabs_mla):
    nb, s, d = x.shape
    h1 = _ffn_ln(x.reshape(nb * s, d), w["ffn1_w_gate"], w["ffn1_w_up"], w["ffn1_w_down"], w["ln1_g"], w["ln1_b"])
    qdat, kda, vdat, qmt, km, vmt = _mixer_inputs(h1, nb, s, w, tabs_da, tabs_mla)
    kda_m, vdat_m, km_m, vmt_m = meta_kv
    lam_vecs = (w["da_lambda_q1"], w["da_lambda_k1"], w["da_lambda_q2"], w["da_lambda_k2"])
    oa = _da_attn(qdat, kda, vdat, kda_m, vdat_m, lam_vecs, w["da_subln_g"])
    om = _mla_attn(qmt, km, vmt, km_m, vmt_m)
    h2 = _outproj_ln(h1, oa.reshape(nb * s, DA_WIDTH), om.reshape(nb * s, MLA_WIDTH), w["w_o"], w["ln2_g"], w["ln2_b"])
    y = _ffn_ln(h2, w["ffn2_w_gate"], w["ffn2_w_up"], w["ffn2_w_down"], w["ln3_g"], w["ln3_b"])
    return y.reshape(nb, s, d)


def kernel(x_prompt, x_sample, meta_tokens, ffn1_w_gate, ffn1_w_up, ffn1_w_down, ln1_g, ln1_b, w_in, da_lambda_q1, da_lambda_k1, da_lambda_q2, da_lambda_k2, da_subln_g, mla_q_norm_g, w_uq, mla_kv_norm_g, w_ukv, w_o, ln2_g, ln2_b, ffn2_w_gate, ffn2_w_up, ffn2_w_down, ln3_g, ln3_b):
    p = dict(ffn1_w_gate=ffn1_w_gate, ffn1_w_up=ffn1_w_up, ffn1_w_down=ffn1_w_down, ln1_g=ln1_g, ln1_b=ln1_b,
             w_in=w_in, da_lambda_q1=da_lambda_q1, da_lambda_k1=da_lambda_k1, da_lambda_q2=da_lambda_q2,
             da_lambda_k2=da_lambda_k2, da_subln_g=da_subln_g, mla_q_norm_g=mla_q_norm_g, w_uq=w_uq,
             mla_kv_norm_g=mla_kv_norm_g, w_ukv=w_ukv, w_o=w_o, ln2_g=ln2_g, ln2_b=ln2_b,
             ffn2_w_gate=ffn2_w_gate, ffn2_w_up=ffn2_w_up, ffn2_w_down=ffn2_w_down, ln3_g=ln3_g, ln3_b=ln3_b)
    w = _prep_weights(p)
    s_max = max(x_prompt.shape[1], x_sample.shape[1])
    tabs_da = _rope_lane_tables(N_META + s_max, DA_ROT, DA_HEAD_DIM)
    tabs_mla = _rope_lane_tables(N_META + s_max, MLA_ROPE, LANES)

    hm = _ffn_ln(meta_tokens.astype(F32), w["ffn1_w_gate"], w["ffn1_w_up"], w["ffn1_w_down"], w["ln1_g"], w["ln1_b"])
    hm = jnp.pad(hm, ((0, PROJ_ROWS - N_META), (0, 0)))
    _, kda_m, vdat_m, _, km_m, vmt_m = _mixer_inputs(
        hm, 1, PROJ_ROWS, w, [t[:PROJ_ROWS] for t in tabs_da], [t[:PROJ_ROWS] for t in tabs_mla])
    meta_kv = (kda_m[:, :, :META_PAD], vdat_m[:, :, 0, :, :META_PAD], km_m[:, :, :META_PAD], vmt_m[:, :, 0, :, :META_PAD])

    seq_da = [t[N_META:] for t in tabs_da]
    seq_mla = [t[N_META:] for t in tabs_mla]
    y_prompt = _trunk(x_prompt, meta_kv, w, seq_da, seq_mla)
    y_sample = _trunk(x_sample, meta_kv, w, seq_da, seq_mla)
    return (y_prompt, y_sample)
```

```python
import functools
import math

import jax
import jax.numpy as jnp
from jax import lax
from jax.experimental import pallas as pl
from jax.experimental.pallas import tpu as pltpu

N_META = 16
ROPE_THETA = 500000.0
DA_HEADS = 8
DA_HEAD_DIM = 64
DA_V_DIM = 2 * DA_HEAD_DIM
DA_ROT = DA_HEAD_DIM // 4
DA_QK_COLS = DA_HEADS * 2 * DA_HEAD_DIM
DA_WIDTH = DA_HEADS * DA_V_DIM
MLA_HEADS = 8
MLA_Q_RANK = 512
MLA_KV_RANK = 256
MLA_NOPE = 128
MLA_ROPE = 64
MLA_V = 128
MLA_WIDTH = MLA_HEADS * MLA_V
MLA_QK_PAD = 256
DEPTH = 1
ALPHA = (2 * DEPTH) ** 0.25
LN_EPS = 1e-5
RMS_EPS = 1e-6
LAMBDA_INIT = 0.8 - 0.6 * math.exp(-0.3 * 0)
LOG2E = 1.4426950408889634

LANES = 128
SUBLANES = 8
META_PAD = 128
NEG_BIG = -1e30

PROJ_ROWS = 256
DA_Q_TILE = 512
MLA_Q_TILE = 1024
KV_CHUNK = 512
CHUNKS_PER_ITER = 4
V_EXTRA_ROWS = 16
V_ROWS = DA_V_DIM + V_EXTRA_ROWS

F32 = jnp.float32
BF16 = jnp.bfloat16
MIB = 1024 * 1024


def _cparams(n_axes, vmem_mib):
    return pltpu.CompilerParams(
        dimension_semantics=("arbitrary",) * n_axes,
        vmem_limit_bytes=int(vmem_mib * MIB),
    )


def _resident(shape):
    n = len(shape)
    return pl.BlockSpec(shape, lambda *_: (0,) * n, pipeline_mode=pl.Buffered(1))


def _layer_norm(y, g, b):
    mu = jnp.mean(y, axis=-1, keepdims=True)
    d = y - mu
    var = jnp.mean(d * d, axis=-1, keepdims=True)
    return d * lax.rsqrt(var + LN_EPS) * g + b


def _rms_norm(x, g):
    return x * lax.rsqrt(jnp.mean(x * x, axis=-1, keepdims=True) + RMS_EPS) * g


def _ffn_ln_kernel(x_ref, wg_ref, wu_ref, wd_ref, g_ref, b_ref, o_ref, xb_ref, acc_ref):
    j = pl.program_id(1)

    @pl.when(j == 0)
    def _():
        xb_ref[...] = x_ref[...].astype(BF16)
        acc_ref[...] = jnp.zeros_like(acc_ref)

    xb = xb_ref[...]
    gate = jnp.dot(xb, wg_ref[...], preferred_element_type=F32)
    up = jnp.dot(xb, wu_ref[...], preferred_element_type=F32)
    hid = (gate * (1.0 / (1.0 + jnp.exp(-gate)))) * up
    acc_ref[...] += jnp.dot(hid.astype(BF16), wd_ref[...], preferred_element_type=F32)

    @pl.when(j == pl.num_programs(1) - 1)
    def _():
        y = ALPHA * x_ref[...] + 0.5 * acc_ref[...]
        o_ref[...] = _layer_norm(y, g_ref[...], b_ref[...])


def _ffn_ln(x, wg, wu, wd, g, b, *, tm=512, tf=512):
    m, d = x.shape
    f = wg.shape[1]
    tm = min(tm, m)
    tf = min(tf, f)
    assert m % tm == 0 and f % tf == 0
    vmem = (2 * 2 * tm * d * 4 + 2 * 3 * d * tf * 2 + tm * d * 6 + 4 * tm * tf * 4) / MIB + 6
    return pl.pallas_call(
        _ffn_ln_kernel,
        grid=(m // tm, f // tf),
        in_specs=[
            pl.BlockSpec((tm, d), lambda i, j: (i, 0)),
            pl.BlockSpec((d, tf), lambda i, j: (0, j)),
            pl.BlockSpec((d, tf), lambda i, j: (0, j)),
            pl.BlockSpec((tf, d), lambda i, j: (j, 0)),
            pl.BlockSpec((1, d), lambda i, j: (0, 0)),
            pl.BlockSpec((1, d), lambda i, j: (0, 0)),
        ],
        out_specs=pl.BlockSpec((tm, d), lambda i, j: (i, 0)),
        out_shape=jax.ShapeDtypeStruct((m, d), F32),
        scratch_shapes=[pltpu.VMEM((tm, d), BF16), pltpu.VMEM((tm, d), F32)],
        compiler_params=_cparams(2, vmem),
        name="ffn_ln",
    )(x, wg, wu, wd, g, b)


def _rope(x, c, sp, sm, half):
    return x * c + pltpu.roll(x, half, 1) * sp + pltpu.roll(x, LANES - half, 1) * sm


def _inproj_kernel(h_ref, win_ref, gq_ref, wuq_ref, gkv_ref, wukv_ref,
                   cda_ref, spda_ref, smda_ref, cm_ref, spm_ref, smm_ref,
                   qdat_ref, kda_ref, vdat_ref, qmt_ref, km_ref, vmt_ref):
    tm = h_ref.shape[0]
    hb = h_ref[...].astype(BF16)

    def proj(lo, hi):
        return jnp.dot(hb, win_ref[:, lo:hi], preferred_element_type=F32)

    cda, spda, smda = cda_ref[...], spda_ref[...], smda_ref[...]
    cm, spm, smm = cm_ref[...], spm_ref[...], smm_ref[...]
    da_half = DA_ROT // 2
    mla_half = MLA_ROPE // 2
    map1_dims = lax.broadcasted_iota(jnp.int32, (LANES, tm), 0) < DA_HEAD_DIM
    da_scale = DA_HEAD_DIM ** -0.5 * LOG2E
    mla_scale = (MLA_NOPE + MLA_ROPE) ** -0.5 * LOG2E
    sum_rows = (lax.broadcasted_iota(jnp.int32, (V_EXTRA_ROWS, tm), 0) == 0).astype(BF16)

    c0 = 0
    zq = proj(c0, c0 + DA_QK_COLS)
    for h in range(DA_HEADS):
        rt = (_rope(zq[:, h * LANES:(h + 1) * LANES], cda, spda, smda, da_half) * da_scale).T
        qdat_ref[0, h, 0, 0] = jnp.where(map1_dims, rt, 0.0).astype(BF16)
        qdat_ref[0, h, 0, 1] = jnp.where(map1_dims, 0.0, rt).astype(BF16)
    c0 += DA_QK_COLS
    zk = proj(c0, c0 + DA_QK_COLS)
    for h in range(DA_HEADS):
        kda_ref[0, h] = _rope(zk[:, h * LANES:(h + 1) * LANES], cda, spda, smda, da_half).astype(BF16)
    c0 += DA_QK_COLS
    zv = proj(c0, c0 + DA_WIDTH)
    for h in range(DA_HEADS):
        vdat_ref[0, h, 0, 0:DA_V_DIM, :] = zv[:, h * DA_V_DIM:(h + 1) * DA_V_DIM].T.astype(BF16)
        vdat_ref[0, h, 0, DA_V_DIM:V_ROWS, :] = sum_rows
    c0 += DA_WIDTH

    cq = _rms_norm(proj(c0, c0 + MLA_Q_RANK), gq_ref[...]).astype(BF16)
    c0 += MLA_Q_RANK
    qm = jnp.dot(cq, wuq_ref[...], preferred_element_type=F32) * mla_scale
    for h in range(MLA_HEADS):
        base = h * MLA_QK_PAD
        qmt_ref[0, h, 0, 0:MLA_NOPE, :] = qm[:, base:base + MLA_NOPE].T.astype(BF16)
        qmt_ref[0, h, 0, MLA_NOPE:MLA_QK_PAD, :] = _rope(
            qm[:, base + MLA_NOPE:base + MLA_QK_PAD], cm, spm, smm, mla_half).T.astype(BF16)

    ckv = _rms_norm(proj(c0, c0 + MLA_KV_RANK), gkv_ref[...]).astype(BF16)
    c0 += MLA_KV_RANK
    kv = jnp.dot(ckv, wukv_ref[...], preferred_element_type=F32)
    kr = _rope(proj(c0, c0 + LANES), cm, spm, smm, mla_half).astype(BF16)
    for h in range(MLA_HEADS):
        base = h * (MLA_NOPE + MLA_V)
        km_ref[0, h, :, 0:MLA_NOPE] = kv[:, base:base + MLA_NOPE].astype(BF16)
        km_ref[0, h, :, MLA_NOPE:MLA_QK_PAD] = kr
        vmt_ref[0, h, 0, 0:MLA_V, :] = kv[:, base + MLA_NOPE:base + MLA_NOPE + MLA_V].T.astype(BF16)
        vmt_ref[0, h, 0, MLA_V:V_ROWS, :] = sum_rows


def _inproj(h, nb, s, win, gq, wuq, gkv, wukv, tabs_da, tabs_mla):
    m, d = h.shape
    tm = PROJ_ROWS
    ck = min(KV_CHUNK, s)
    tqd = min(DA_Q_TILE, s)
    tqm = min(MLA_Q_TILE, s)
    assert m == nb * s and all(t % tm == 0 and s % t == 0 for t in (tm, ck, tqd, tqm))
    spb = s // tm
    tab_spec = pl.BlockSpec((tm, LANES), lambda i: (i % spb, 0))

    def rows_spec(width):
        return pl.BlockSpec((1, DA_HEADS, tm, width), lambda i: (i // spb, 0, i % spb, 0))

    def cols_spec(rows, tile, *mid):
        per = tile // tm
        zeros = (0,) * (len(mid) + 1)
        return pl.BlockSpec((1, DA_HEADS, 1) + mid + (rows, tm),
                            lambda i: (i // spb, 0, (i % spb) // per) + zeros + ((i % spb) % per,))

    def shape(*tail):
        return jax.ShapeDtypeStruct((nb, DA_HEADS) + tail, BF16)

    vmem = (2 * tm * d * 4 + (win.size + wuq.size + wukv.size) * 2 + 2 * 6 * tm * LANES * 4
            + 2 * tm * 8 * (256 + 128 + 128 + 256 + 256 + 128) * 2 + 8 * tm * 2048 * 4) / MIB + 6
    return pl.pallas_call(
        _inproj_kernel,
        grid=(m // tm,),
        in_specs=[
            pl.BlockSpec((tm, d), lambda i: (i, 0)),
            _resident(win.shape), _resident(gq.shape), _resident(wuq.shape),
            _resident(gkv.shape), _resident(wukv.shape),
            tab_spec, tab_spec, tab_spec, tab_spec, tab_spec, tab_spec,
        ],
        out_specs=[
            cols_spec(LANES, tqd, 2),
            rows_spec(LANES), cols_spec(V_ROWS, ck),
            cols_spec(MLA_QK_PAD, tqm), rows_spec(MLA_QK_PAD), cols_spec(V_ROWS, ck),
        ],
        out_shape=[
            shape(s // tqd, 2, LANES, tqd), shape(s, LANES), shape(s // ck, V_ROWS, ck),
            shape(s // tqm, MLA_QK_PAD, tqm), shape(s, MLA_QK_PAD), shape(s // ck, V_ROWS, ck),
        ],
        compiler_params=_cparams(1, vmem),
        name="inproj",
    )(h, win, gq, wuq, gkv, wukv, *tabs_da, *tabs_mla)


def _flash_cols(qt, k_ref, vt_ref, kmeta, vmeta_t, s_scrs, cmax_scrs, m_scr, acc_scr):
    n_chunks, tk = vt_ref.shape[2], vt_ref.shape[4]
    r = qt.shape[1]
    group = CHUNKS_PER_ITER
    assert group % 2 == 0

    def bcast(x):
        return jnp.broadcast_to(x, (SUBLANES, r))

    def scores(c):
        off = c * tk if isinstance(c, int) else pl.multiple_of(c * tk, tk)
        return jnp.dot(k_ref[0, 0, pl.ds(off, tk), :], qt, preferred_element_type=F32)

    def put(s, slot, rows):
        s_scrs[slot][0:rows, :] = s
        cmax_scrs[slot][...] = bcast(jnp.max(s, axis=0, keepdims=True))

    def fold(slot, rows, vt, first=False):
        cmax = cmax_scrs[slot][0:1, :]
        m_new = cmax if first else jnp.maximum(m_scr[0:1, :], cmax)
        p = jnp.exp2(s_scrs[slot][0:rows, :] - m_new)
        pv = jnp.dot(vt, p.astype(BF16), preferred_element_type=F32)
        if first:
            acc_scr[...] = pv
        else:
            acc_scr[...] = jnp.exp2(m_scr[0:1, :] - m_new) * acc_scr[...] + pv
        m_scr[...] = bcast(m_new)

    s = jnp.dot(kmeta, qt, preferred_element_type=F32)
    put(jnp.where(lax.broadcasted_iota(jnp.int32, s.shape, 0) < N_META, s, NEG_BIG), 1, META_PAD)
    put(scores(0), 0, tk)
    fold(1, META_PAD, vmeta_t, first=True)

    def step(c, parity, last):
        if not last:
            put(scores(c + 1), 1 - parity, tk)
        fold(parity, tk, vt_ref[0, 0, c])

    n_iter = (n_chunks - 1) // group

    def body(j, carry):
        for u in range(group):
            step(group * j + u, u % 2, False)
        return carry

    lax.fori_loop(0, n_iter, body, 0)
    for c in range(group * n_iter, n_chunks):
        step(c, c % 2, c == n_chunks - 1)


def _normalised(acc_scr, dv):
    return acc_scr[0:dv, :] * (1.0 / acc_scr[dv:dv + 1, :])


def _da_attn_kernel(qt_ref, k_ref, vt_ref, kmeta_ref, vmeta_ref, lq1_ref, lk1_ref, lq2_ref, lk2_ref,
                    g_ref, o_ref, s0_scr, s1_scr, c0_scr, c1_scr, m_scr, acc_scr):
    tq = o_ref.shape[1]
    qt = jnp.concatenate([qt_ref[0, 0, 0, 0], qt_ref[0, 0, 0, 1]], axis=1)
    _flash_cols(qt, k_ref, vt_ref, kmeta_ref[0, 0], vmeta_ref[0, 0],
                (s0_scr, s1_scr), (c0_scr, c1_scr), m_scr, acc_scr)
    at = _normalised(acc_scr, DA_V_DIM)
    lam = (jnp.exp(jnp.sum(lq1_ref[...] * lk1_ref[...], axis=1, keepdims=True))
           - jnp.exp(jnp.sum(lq2_ref[...] * lk2_ref[...], axis=1, keepdims=True)) + LAMBDA_INIT)
    o = (at[:, :tq] - lam * at[:, tq:]).T
    o_ref[0] = (_rms_norm(o, g_ref[...]) * (1.0 - LAMBDA_INIT)).astype(BF16)


def _mla_attn_kernel(qt_ref, k_ref, vt_ref, kmeta_ref, vmeta_ref, o_ref,
                     s0_scr, s1_scr, c0_scr, c1_scr, m_scr, acc_scr):
    _flash_cols(qt_ref[0, 0, 0], k_ref, vt_ref, kmeta_ref[0, 0], vmeta_ref[0, 0],
                (s0_scr, s1_scr), (c0_scr, c1_scr), m_scr, acc_scr)
    o_ref[0] = _normalised(acc_scr, MLA_V).T.astype(BF16)


def _attn_scratch(cols, tk):
    stat = pltpu.VMEM((SUBLANES, cols), F32)
    return [pltpu.VMEM((tk, cols), F32), pltpu.VMEM((tk, cols), F32), stat, stat, stat,
            pltpu.VMEM((V_ROWS, cols), F32)]


def _attn_vmem(cols, tk, s, dk, dv):
    return (2 * s * (dk + dv) * 2 + 2 * dk * cols * 2 + 2 * tk * cols * 4 + dv * cols * 4
            + 4 * tk * cols * 4 + 2 * cols * dv * 2) / MIB + 8


def _da_attn(qdat, kda, vdat, kmeta, vmeta_t, lam_vecs, g):
    nb, nh, nq, _, _, tq = qdat.shape
    cols = 2 * tq
    s = kda.shape[2]
    nc, tk = vdat.shape[2], vdat.shape[4]
    vec_spec = pl.BlockSpec((1, DA_HEAD_DIM), lambda b, h, i: (0, 0))
    meta_spec = pl.BlockSpec((1, 1, META_PAD, LANES), lambda b, h, i: (0, h, 0, 0))
    vmeta_spec = pl.BlockSpec((1, 1, V_ROWS, META_PAD), lambda b, h, i: (0, h, 0, 0))
    return pl.pallas_call(
        _da_attn_kernel,
        grid=(nb, nh, nq),
        in_specs=[
            pl.BlockSpec((1, 1, 1, 2, LANES, tq), lambda b, h, i: (b, h, i, 0, 0, 0)),
            pl.BlockSpec((1, 1, s, LANES), lambda b, h, i: (b, h, 0, 0)),
            pl.BlockSpec((1, 1, nc, V_ROWS, tk), lambda b, h, i: (b, h, 0, 0, 0)),
            meta_spec, vmeta_spec,
            vec_spec, vec_spec, vec_spec, vec_spec,
            pl.BlockSpec((1, DA_V_DIM), lambda b, h, i: (0, 0)),
        ],
        out_specs=pl.BlockSpec((1, tq, DA_V_DIM), lambda b, h, i: (b, i, h)),
        out_shape=jax.ShapeDtypeStruct((nb, s, DA_WIDTH), BF16),
        scratch_shapes=_attn_scratch(cols, tk),
        compiler_params=_cparams(3, _attn_vmem(cols, tk, s, LANES, DA_V_DIM)),
        name="da_attn",
    )(qdat, kda, vdat, kmeta, vmeta_t, *lam_vecs, g)


def _mla_attn(qmt, km, vmt, kmeta, vmeta_t):
    nb, nh, nq, _, tq = qmt.shape
    s = km.shape[2]
    nc, tk = vmt.shape[2], vmt.shape[4]
    return pl.pallas_call(
        _mla_attn_kernel,
        grid=(nb, nh, nq),
        in_specs=[
            pl.BlockSpec((1, 1, 1, MLA_QK_PAD, tq), lambda b, h, i: (b, h, i, 0, 0)),
            pl.BlockSpec((1, 1, s, MLA_QK_PAD), lambda b, h, i: (b, h, 0, 0)),
            pl.BlockSpec((1, 1, nc, V_ROWS, tk), lambda b, h, i: (b, h, 0, 0, 0)),
            pl.BlockSpec((1, 1, META_PAD, MLA_QK_PAD), lambda b, h, i: (0, h, 0, 0)),
            pl.BlockSpec((1, 1, V_ROWS, META_PAD), lambda b, h, i: (0, h, 0, 0)),
        ],
        out_specs=pl.BlockSpec((1, tq, MLA_V), lambda b, h, i: (b, i, h)),
        out_shape=jax.ShapeDtypeStruct((nb, s, MLA_WIDTH), BF16),
        scratch_shapes=_attn_scratch(tq, tk),
        compiler_params=_cparams(3, _attn_vmem(tq, tk, s, MLA_QK_PAD, MLA_V)),
        name="mla_attn",
    )(qmt, km, vmt, kmeta, vmeta_t)


def _outproj_ln_kernel(h_ref, oa_ref, om_ref, wo_ref, g_ref, b_ref, o_ref):
    mix = (jnp.dot(oa_ref[...], wo_ref[0:DA_WIDTH, :], preferred_element_type=F32)
           + jnp.dot(om_ref[...], wo_ref[DA_WIDTH:DA_WIDTH + MLA_WIDTH, :], preferred_element_type=F32))
    o_ref[...] = _layer_norm(ALPHA * h_ref[...] + mix, g_ref[...], b_ref[...])


def _outproj_ln(h, oa, om, wo, g, b, *, tm=512):
    m, d = h.shape
    tm = min(tm, m)
    assert m % tm == 0
    vmem = (2 * 2 * tm * d * 4 + 2 * 2 * tm * DA_WIDTH * 2 + wo.size * 2 + 3 * tm * d * 4) / MIB + 6
    return pl.pallas_call(
        _outproj_ln_kernel,
        grid=(m // tm,),
        in_specs=[
            pl.BlockSpec((tm, d), lambda i: (i, 0)),
            pl.BlockSpec((tm, DA_WIDTH), lambda i: (i, 0)),
            pl.BlockSpec((tm, MLA_WIDTH), lambda i: (i, 0)),
            _resident(wo.shape), _resident(g.shape), _resident(b.shape),
        ],
        out_specs=pl.BlockSpec((tm, d), lambda i: (i, 0)),
        out_shape=jax.ShapeDtypeStruct((m, d), F32),
        compiler_params=_cparams(1, vmem),
        name="outproj_ln",
    )(h, oa, om, wo, g, b)


def _rope_lane_tables(n_pos, rot_dim, period):
    half = rot_dim // 2
    inv = 1.0 / (ROPE_THETA ** (jnp.arange(0, rot_dim, 2, dtype=F32) / rot_dim))
    ang = jnp.arange(n_pos, dtype=F32)[:, None] * inv[None, :]
    cos, sin = jnp.cos(ang), jnp.sin(ang)
    within = jnp.arange(LANES) % period
    idx = within % half
    c = jnp.where(within < rot_dim, cos[:, idx], 1.0)
    sp = jnp.where((within >= half) & (within < rot_dim), sin[:, idx], 0.0)
    sm = jnp.where(within < half, -sin[:, idx], 0.0)
    return c, sp, sm


def _prep_weights(p):
    w = {}
    for name in ("ffn1_w_gate", "ffn1_w_up", "ffn1_w_down", "ffn2_w_gate", "ffn2_w_up", "ffn2_w_down", "w_o", "w_ukv"):
        w[name] = p[name][0].astype(BF16)
    w_in = p["w_in"][0]
    w["w_in"] = jnp.pad(w_in, ((0, 0), (0, LANES - MLA_ROPE))).astype(BF16)
    w_uq = p["w_uq"][0].reshape(MLA_Q_RANK, MLA_HEADS, MLA_NOPE + MLA_ROPE)
    w_uq = jnp.pad(w_uq, ((0, 0), (0, 0), (0, MLA_QK_PAD - MLA_NOPE - MLA_ROPE)))
    w["w_uq"] = w_uq.reshape(MLA_Q_RANK, MLA_HEADS * MLA_QK_PAD).astype(BF16)
    for name in ("ln1_g", "ln1_b", "ln2_g", "ln2_b", "ln3_g", "ln3_b", "da_subln_g", "mla_q_norm_g", "mla_kv_norm_g",
                 "da_lambda_q1", "da_lambda_k1", "da_lambda_q2", "da_lambda_k2"):
        w[name] = p[name].astype(F32)
    return w


def _mixer_inputs(h, nb, s, w, tabs_da, tabs_mla):
    return _inproj(h, nb, s, w["w_in"], w["mla_q_norm_g"], w["w_uq"], w["mla_kv_norm_g"], w["w_ukv"],
                   tabs_da, tabs_mla)


def _trunk(x, meta_kv, w, tabs_da, tabs_mla):
    nb, s, d = x.shape
    h1 = _ffn_ln(x.reshape(nb * s, d), w["ffn1_w_gate"], w["ffn1_w_up"], w["ffn1_w_down"], w["ln1_g"], w["ln1_b"])
    qdat, kda, vdat, qmt, km, vmt = _mixer_inputs(h1, nb, s, w, tabs_da, tabs_mla)
    kda_m, vdat_m, km_m, vmt_m = meta_kv
    lam_vecs = (w["da_lambda_q1"], w["da_lambda_k1"], w["da_lambda_q2"], w["da_lambda_k2"])
    oa = _da_attn(qdat, kda, vdat, kda_m, vdat_m, lam_vecs, w["da_subln_g"])
    om = _mla_attn(qmt, km, vmt, km_m, vmt_m)
    h2 = _outproj_ln(h1, oa.reshape(nb * s, DA_WIDTH), om.reshape(nb * s, MLA_WIDTH), w["w_o"], w["ln2_g"], w["ln2_b"])
    y = _ffn_ln(h2, w["ffn2_w_gate"], w["ffn2_w_up"], w["ffn2_w_down"], w["ln3_g"], w["ln3_b"])
    return y.reshape(nb, s, d)


def kernel(x_prompt, x_sample, meta_tokens, ffn1_w_gate, ffn1_w_up, ffn1_w_down, ln1_g, ln1_b, w_in, da_lambda_q1, da_lambda_k1, da_lambda_q2, da_lambda_k2, da_subln_g, mla_q_norm_g, w_uq, mla_kv_norm_g, w_ukv, w_o, ln2_g, ln2_b, ffn2_w_gate, ffn2_w_up, ffn2_w_down, ln3_g, ln3_b):
    p = dict(ffn1_w_gate=ffn1_w_gate, ffn1_w_up=ffn1_w_up, ffn1_w_down=ffn1_w_down, ln1_g=ln1_g, ln1_b=ln1_b,
             w_in=w_in, da_lambda_q1=da_lambda_q1, da_lambda_k1=da_lambda_k1, da_lambda_q2=da_lambda_q2,
             da_lambda_k2=da_lambda_k2, da_subln_g=da_subln_g, mla_q_norm_g=mla_q_norm_g, w_uq=w_uq,
             mla_kv_norm_g=mla_kv_norm_g, w_ukv=w_ukv, w_o=w_o, ln2_g=ln2_g, ln2_b=ln2_b,
             ffn2_w_gate=ffn2_w_gate, ffn2_w_up=ffn2_w_up, ffn2_w_down=ffn2_w_down, ln3_g=ln3_g, ln3_b=ln3_b)
    w = _prep_weights(p)
    s_max = max(x_prompt.shape[1], x_sample.shape[1])
    tabs_da = _rope_lane_tables(N_META + s_max, DA_ROT, DA_HEAD_DIM)
    tabs_mla = _rope_lane_tables(N_META + s_max, MLA_ROPE, LANES)

    hm = _ffn_ln(meta_tokens.astype(F32), w["ffn1_w_gate"], w["ffn1_w_up"], w["ffn1_w_down"], w["ln1_g"], w["ln1_b"])
    hm = jnp.pad(hm, ((0, PROJ_ROWS - N_META), (0, 0)))
    _, kda_m, vdat_m, _, km_m, vmt_m = _mixer_inputs(
        hm, 1, PROJ_ROWS, w, [t[:PROJ_ROWS] for t in tabs_da], [t[:PROJ_ROWS] for t in tabs_mla])
    meta_kv = (kda_m[:, :, :META_PAD], vdat_m[:, :, 0, :, :META_PAD], km_m[:, :, :META_PAD], vmt_m[:, :, 0, :, :META_PAD])

    seq_da = [t[N_META:] for t in tabs_da]
    seq_mla = [t[N_META:] for t in tabs_mla]
    y_prompt = _trunk(x_prompt, meta_kv, w, seq_da, seq_mla)
    y_sample = _trunk(x_sample, meta_kv, w, seq_da, seq_mla)
    return (y_prompt, y_sample)
```

```python
import functools
import math

import jax
import jax.numpy as jnp
from jax import lax
from jax.experimental import pallas as pl
from jax.experimental.pallas import tpu as pltpu

N_META = 16
ROPE_THETA = 500000.0
DA_HEADS = 8
DA_HEAD_DIM = 64
DA_V_DIM = 2 * DA_HEAD_DIM
DA_ROT = DA_HEAD_DIM // 4
DA_QK_COLS = DA_HEADS * 2 * DA_HEAD_DIM
DA_WIDTH = DA_HEADS * DA_V_DIM
MLA_HEADS = 8
MLA_Q_RANK = 512
MLA_KV_RANK = 256
MLA_NOPE = 128
MLA_ROPE = 64
MLA_V = 128
MLA_WIDTH = MLA_HEADS * MLA_V
MLA_QK_PAD = 256
DEPTH = 1
ALPHA = (2 * DEPTH) ** 0.25
LN_EPS = 1e-5
RMS_EPS = 1e-6
LAMBDA_INIT = 0.8 - 0.6 * math.exp(-0.3 * 0)
LOG2E = 1.4426950408889634

LANES = 128
SUBLANES = 8
META_PAD = 128
NEG_BIG = -1e30

PROJ_ROWS = 256
DA_Q_TILE = 512
MLA_Q_TILE = 1024
KV_CHUNK = 512
CHUNKS_PER_ITER = 8
ONLINE_CHUNKS_PER_ITER = 2
SCORE_BOUND = 45.0
BOUND_SLACK = 1.001
V_EXTRA_ROWS = 16
V_ROWS = DA_V_DIM + V_EXTRA_ROWS

F32 = jnp.float32
BF16 = jnp.bfloat16
MIB = 1024 * 1024


def _cparams(n_axes, vmem_mib):
    return pltpu.CompilerParams(
        dimension_semantics=("arbitrary",) * n_axes,
        vmem_limit_bytes=int(vmem_mib * MIB),
    )


def _resident(shape):
    n = len(shape)
    return pl.BlockSpec(shape, lambda *_: (0,) * n, pipeline_mode=pl.Buffered(1))


def _layer_norm(y, g, b):
    mu = jnp.mean(y, axis=-1, keepdims=True)
    d = y - mu
    var = jnp.mean(d * d, axis=-1, keepdims=True)
    return d * lax.rsqrt(var + LN_EPS) * g + b


def _rms_norm(x, g):
    return x * lax.rsqrt(jnp.mean(x * x, axis=-1, keepdims=True) + RMS_EPS) * g


def _ffn_ln_kernel(x_ref, wg_ref, wu_ref, wd_ref, g_ref, b_ref, o_ref, xb_ref, acc_ref):
    j = pl.program_id(1)

    @pl.when(j == 0)
    def _():
        xb_ref[...] = x_ref[...].astype(BF16)
        acc_ref[...] = jnp.zeros_like(acc_ref)

    xb = xb_ref[...]
    gate = jnp.dot(xb, wg_ref[...], preferred_element_type=F32)
    up = jnp.dot(xb, wu_ref[...], preferred_element_type=F32)
    hid = (gate * (1.0 / (1.0 + jnp.exp(-gate)))) * up
    acc_ref[...] += jnp.dot(hid.astype(BF16), wd_ref[...], preferred_element_type=F32)

    @pl.when(j == pl.num_programs(1) - 1)
    def _():
        y = ALPHA * x_ref[...] + 0.5 * acc_ref[...]
        o_ref[...] = _layer_norm(y, g_ref[...], b_ref[...])


def _ffn_ln(x, wg, wu, wd, g, b, *, tm=512, tf=512):
    m, d = x.shape
    f = wg.shape[1]
    tm = min(tm, m)
    tf = min(tf, f)
    assert m % tm == 0 and f % tf == 0
    vmem = (2 * 2 * tm * d * 4 + 2 * 3 * d * tf * 2 + tm * d * 6 + 4 * tm * tf * 4) / MIB + 6
    return pl.pallas_call(
        _ffn_ln_kernel,
        grid=(m // tm, f // tf),
        in_specs=[
            pl.BlockSpec((tm, d), lambda i, j: (i, 0)),
            pl.BlockSpec((d, tf), lambda i, j: (0, j)),
            pl.BlockSpec((d, tf), lambda i, j: (0, j)),
            pl.BlockSpec((tf, d), lambda i, j: (j, 0)),
            pl.BlockSpec((1, d), lambda i, j: (0, 0)),
            pl.BlockSpec((1, d), lambda i, j: (0, 0)),
        ],
        out_specs=pl.BlockSpec((tm, d), lambda i, j: (i, 0)),
        out_shape=jax.ShapeDtypeStruct((m, d), F32),
        scratch_shapes=[pltpu.VMEM((tm, d), BF16), pltpu.VMEM((tm, d), F32)],
        compiler_params=_cparams(2, vmem),
        name="ffn_ln",
    )(x, wg, wu, wd, g, b)


def _rope(x, c, sp, sm, half):
    return x * c + pltpu.roll(x, half, 1) * sp + pltpu.roll(x, LANES - half, 1) * sm


def _inproj_kernel(h_ref, win_ref, gq_ref, wuq_ref, gkv_ref, wukv_ref,
                   cda_ref, spda_ref, smda_ref, cm_ref, spm_ref, smm_ref,
                   qdat_ref, kda_ref, vdat_ref, qmt_ref, km_ref, vmt_ref):
    tm = h_ref.shape[0]
    hb = h_ref[...].astype(BF16)

    def proj(lo, hi):
        return jnp.dot(hb, win_ref[:, lo:hi], preferred_element_type=F32)

    cda, spda, smda = cda_ref[...], spda_ref[...], smda_ref[...]
    cm, spm, smm = cm_ref[...], spm_ref[...], smm_ref[...]
    da_half = DA_ROT // 2
    mla_half = MLA_ROPE // 2
    map1_dims = lax.broadcasted_iota(jnp.int32, (LANES, tm), 0) < DA_HEAD_DIM
    da_scale = DA_HEAD_DIM ** -0.5 * LOG2E
    mla_scale = (MLA_NOPE + MLA_ROPE) ** -0.5 * LOG2E
    sum_rows = (lax.broadcasted_iota(jnp.int32, (V_EXTRA_ROWS, tm), 0) == 0).astype(BF16)

    c0 = 0
    zq = proj(c0, c0 + DA_QK_COLS)
    for h in range(DA_HEADS):
        rt = (_rope(zq[:, h * LANES:(h + 1) * LANES], cda, spda, smda, da_half) * da_scale).T
        qdat_ref[0, h, 0, 0] = jnp.where(map1_dims, rt, 0.0).astype(BF16)
        qdat_ref[0, h, 0, 1] = jnp.where(map1_dims, 0.0, rt).astype(BF16)
    c0 += DA_QK_COLS
    zk = proj(c0, c0 + DA_QK_COLS)
    for h in range(DA_HEADS):
        kda_ref[0, h] = _rope(zk[:, h * LANES:(h + 1) * LANES], cda, spda, smda, da_half).astype(BF16)
    c0 += DA_QK_COLS
    zv = proj(c0, c0 + DA_WIDTH)
    for h in range(DA_HEADS):
        vdat_ref[0, h, 0, 0:DA_V_DIM, :] = zv[:, h * DA_V_DIM:(h + 1) * DA_V_DIM].T.astype(BF16)
        vdat_ref[0, h, 0, DA_V_DIM:V_ROWS, :] = sum_rows
    c0 += DA_WIDTH

    cq = _rms_norm(proj(c0, c0 + MLA_Q_RANK), gq_ref[...]).astype(BF16)
    c0 += MLA_Q_RANK
    qm = jnp.dot(cq, wuq_ref[...], preferred_element_type=F32) * mla_scale
    for h in range(MLA_HEADS):
        base = h * MLA_QK_PAD
        qmt_ref[0, h, 0, 0:MLA_NOPE, :] = qm[:, base:base + MLA_NOPE].T.astype(BF16)
        qmt_ref[0, h, 0, MLA_NOPE:MLA_QK_PAD, :] = _rope(
            qm[:, base + MLA_NOPE:base + MLA_QK_PAD], cm, spm, smm, mla_half).T.astype(BF16)

    ckv = _rms_norm(proj(c0, c0 + MLA_KV_RANK), gkv_ref[...]).astype(BF16)
    c0 += MLA_KV_RANK
    kv = jnp.dot(ckv, wukv_ref[...], preferred_element_type=F32)
    kr = _rope(proj(c0, c0 + LANES), cm, spm, smm, mla_half).astype(BF16)
    for h in range(MLA_HEADS):
        base = h * (MLA_NOPE + MLA_V)
        km_ref[0, h, :, 0:MLA_NOPE] = kv[:, base:base + MLA_NOPE].astype(BF16)
        km_ref[0, h, :, MLA_NOPE:MLA_QK_PAD] = kr
        vmt_ref[0, h, 0, 0:MLA_V, :] = kv[:, base + MLA_NOPE:base + MLA_NOPE + MLA_V].T.astype(BF16)
        vmt_ref[0, h, 0, MLA_V:V_ROWS, :] = sum_rows


def _inproj(h, nb, s, win, gq, wuq, gkv, wukv, tabs_da, tabs_mla):
    m, d = h.shape
    tm = PROJ_ROWS
    ck = min(KV_CHUNK, s)
    tqd = min(DA_Q_TILE, s)
    tqm = min(MLA_Q_TILE, s)
    assert m == nb * s and all(t % tm == 0 and s % t == 0 for t in (tm, ck, tqd, tqm))
    spb = s // tm
    tab_spec = pl.BlockSpec((tm, LANES), lambda i: (i % spb, 0))

    def rows_spec(width):
        return pl.BlockSpec((1, DA_HEADS, tm, width), lambda i: (i // spb, 0, i % spb, 0))

    def cols_spec(rows, tile, *mid):
        per = tile // tm
        zeros = (0,) * (len(mid) + 1)
        return pl.BlockSpec((1, DA_HEADS, 1) + mid + (rows, tm),
                            lambda i: (i // spb, 0, (i % spb) // per) + zeros + ((i % spb) % per,))

    def shape(*tail):
        return jax.ShapeDtypeStruct((nb, DA_HEADS) + tail, BF16)

    vmem = (2 * tm * d * 4 + (win.size + wuq.size + wukv.size) * 2 + 2 * 6 * tm * LANES * 4
            + 2 * tm * 8 * (256 + 128 + 128 + 256 + 256 + 128) * 2 + 8 * tm * 2048 * 4) / MIB + 6
    return pl.pallas_call(
        _inproj_kernel,
        grid=(m // tm,),
        in_specs=[
            pl.BlockSpec((tm, d), lambda i: (i, 0)),
            _resident(win.shape), _resident(gq.shape), _resident(wuq.shape),
            _resident(gkv.shape), _resident(wukv.shape),
            tab_spec, tab_spec, tab_spec, tab_spec, tab_spec, tab_spec,
        ],
        out_specs=[
            cols_spec(LANES, tqd, 2),
            rows_spec(LANES), cols_spec(V_ROWS, ck),
            cols_spec(MLA_QK_PAD, tqm), rows_spec(MLA_QK_PAD), cols_spec(V_ROWS, ck),
        ],
        out_shape=[
            shape(s // tqd, 2, LANES, tqd), shape(s, LANES), shape(s // ck, V_ROWS, ck),
            shape(s // tqm, MLA_QK_PAD, tqm), shape(s, MLA_QK_PAD), shape(s // ck, V_ROWS, ck),
        ],
        compiler_params=_cparams(1, vmem),
        name="inproj",
    )(h, win, gq, wuq, gkv, wukv, *tabs_da, *tabs_mla)


def _masked_meta_scores(kmeta, qt):
    s = jnp.dot(kmeta, qt, preferred_element_type=F32)
    return jnp.where(lax.broadcasted_iota(jnp.int32, s.shape, 0) < N_META, s, NEG_BIG)


def _max_key_norm(k_ref, kmeta, kmax_scr):
    s_len, dk = k_ref.shape[2], k_ref.shape[3]
    rows = min(KV_CHUNK, s_len)

    def max_sq_norm(x):
        xf = x.astype(F32)
        sq = xf * xf
        part = sq[:, 0:LANES]
        for t in range(1, dk // LANES):
            part = part + sq[:, t * LANES:(t + 1) * LANES]
        return jnp.max(jnp.sum(part, axis=1, keepdims=True), axis=0, keepdims=True)

    def body(c, best):
        off = pl.multiple_of(c * rows, rows)
        return jnp.maximum(best, max_sq_norm(k_ref[0, 0, pl.ds(off, rows), :]))

    best = lax.fori_loop(0, s_len // rows, body, max_sq_norm(kmeta))
    kmax_scr[...] = jnp.broadcast_to(jnp.sqrt(best), kmax_scr.shape)


def _score_shift(qt, kmax_scr):
    qf = qt.astype(F32)
    return jnp.sqrt(jnp.sum(qf * qf, axis=0, keepdims=True)) * (kmax_scr[0:1, 0:1] * BOUND_SLACK)


def _flash_cols_shifted(qt, shift, k_ref, vt_ref, kmeta, vmeta_t, acc_scr):
    n_chunks, tk = vt_ref.shape[2], vt_ref.shape[4]
    group = CHUNKS_PER_ITER

    def weights(s):
        return jnp.exp2(s - shift).astype(BF16)

    def chunk_pv(c):
        off = c * tk if isinstance(c, int) else pl.multiple_of(c * tk, tk)
        s = jnp.dot(k_ref[0, 0, pl.ds(off, tk), :], qt, preferred_element_type=F32)
        return jnp.dot(vt_ref[0, 0, c], weights(s), preferred_element_type=F32)

    acc_scr[...] = jnp.dot(vmeta_t, weights(_masked_meta_scores(kmeta, qt)), preferred_element_type=F32)

    def body(j, carry):
        total = chunk_pv(group * j)
        for u in range(1, group):
            total = total + chunk_pv(group * j + u)
        acc_scr[...] += total
        return carry

    n_iter = n_chunks // group if n_chunks >= 2 * group else 0
    lax.fori_loop(0, n_iter, body, 0)
    for c in range(group * n_iter, n_chunks):
        acc_scr[...] += chunk_pv(c)


def _flash_cols_online(qt, k_ref, vt_ref, kmeta, vmeta_t, s_scrs, cmax_scrs, m_scr, acc_scr):
    n_chunks, tk = vt_ref.shape[2], vt_ref.shape[4]
    r = qt.shape[1]
    group = ONLINE_CHUNKS_PER_ITER
    assert group % 2 == 0

    def bcast(x):
        return jnp.broadcast_to(x, (SUBLANES, r))

    def scores(c):
        off = c * tk if isinstance(c, int) else pl.multiple_of(c * tk, tk)
        return jnp.dot(k_ref[0, 0, pl.ds(off, tk), :], qt, preferred_element_type=F32)

    def put(s, slot, rows):
        s_scrs[slot][0:rows, :] = s
        cmax_scrs[slot][...] = bcast(jnp.max(s, axis=0, keepdims=True))

    def fold(slot, rows, vt, first=False):
        cmax = cmax_scrs[slot][0:1, :]
        m_new = cmax if first else jnp.maximum(m_scr[0:1, :], cmax)
        p = jnp.exp2(s_scrs[slot][0:rows, :] - m_new)
        pv = jnp.dot(vt, p.astype(BF16), preferred_element_type=F32)
        if first:
            acc_scr[...] = pv
        else:
            acc_scr[...] = jnp.exp2(m_scr[0:1, :] - m_new) * acc_scr[...] + pv
        m_scr[...] = bcast(m_new)

    put(_masked_meta_scores(kmeta, qt), 1, META_PAD)
    put(scores(0), 0, tk)
    fold(1, META_PAD, vmeta_t, first=True)

    def step(c, parity, last):
        if not last:
            put(scores(c + 1), 1 - parity, tk)
        fold(parity, tk, vt_ref[0, 0, c])

    n_iter = (n_chunks - 1) // group

    def body(j, carry):
        for u in range(group):
            step(group * j + u, u % 2, False)
        return carry

    lax.fori_loop(0, n_iter, body, 0)
    for c in range(group * n_iter, n_chunks):
        step(c, c % 2, c == n_chunks - 1)


def _normalised(acc_scr, dv):
    return acc_scr[0:dv, :] * (1.0 / acc_scr[dv:dv + 1, :])


def _flash_cols(qt, k_ref, vt_ref, kmeta, vmeta_t, s0_scr, s1_scr, c0_scr, c1_scr, m_scr, kmax_scr, acc_scr):
    @pl.when(pl.program_id(2) == 0)
    def _():
        _max_key_norm(k_ref, kmeta, kmax_scr)

    shift = _score_shift(qt, kmax_scr)
    bounded = jnp.max(shift) <= SCORE_BOUND

    @pl.when(bounded)
    def _():
        _flash_cols_shifted(qt, shift, k_ref, vt_ref, kmeta, vmeta_t, acc_scr)

    @pl.when(jnp.logical_not(bounded))
    def _():
        _flash_cols_online(qt, k_ref, vt_ref, kmeta, vmeta_t, (s0_scr, s1_scr), (c0_scr, c1_scr), m_scr, acc_scr)


def _da_attn_kernel(qt_ref, k_ref, vt_ref, kmeta_ref, vmeta_ref, lq1_ref, lk1_ref, lq2_ref, lk2_ref,
                    g_ref, o_ref, *scratch):
    tq = o_ref.shape[1]
    qt = jnp.concatenate([qt_ref[0, 0, 0, 0], qt_ref[0, 0, 0, 1]], axis=1)
    _flash_cols(qt, k_ref, vt_ref, kmeta_ref[0, 0], vmeta_ref[0, 0], *scratch)
    acc_scr = scratch[-1]
    at = _normalised(acc_scr, DA_V_DIM)
    lam = (jnp.exp(jnp.sum(lq1_ref[...] * lk1_ref[...], axis=1, keepdims=True))
           - jnp.exp(jnp.sum(lq2_ref[...] * lk2_ref[...], axis=1, keepdims=True)) + LAMBDA_INIT)
    o = (at[:, :tq] - lam * at[:, tq:]).T
    o_ref[0] = (_rms_norm(o, g_ref[...]) * (1.0 - LAMBDA_INIT)).astype(BF16)


def _mla_attn_kernel(qt_ref, k_ref, vt_ref, kmeta_ref, vmeta_ref, o_ref, *scratch):
    _flash_cols(qt_ref[0, 0, 0], k_ref, vt_ref, kmeta_ref[0, 0], vmeta_ref[0, 0], *scratch)
    o_ref[0] = _normalised(scratch[-1], MLA_V).T.astype(BF16)


def _attn_scratch(cols, tk):
    stat = pltpu.VMEM((SUBLANES, cols), F32)
    return [pltpu.VMEM((tk, cols), F32), pltpu.VMEM((tk, cols), F32), stat, stat, stat,
            pltpu.VMEM((SUBLANES, LANES), F32), pltpu.VMEM((V_ROWS, cols), F32)]


def _attn_vmem(cols, tk, s, dk, dv):
    return (2 * s * (dk + dv) * 2 + 2 * dk * cols * 2 + 2 * tk * cols * 4 + dv * cols * 4
            + 4 * tk * cols * 4 + 2 * cols * dv * 2) / MIB + 8


def _da_attn(qdat, kda, vdat, kmeta, vmeta_t, lam_vecs, g):
    nb, nh, nq, _, _, tq = qdat.shape
    cols = 2 * tq
    s = kda.shape[2]
    nc, tk = vdat.shape[2], vdat.shape[4]
    vec_spec = pl.BlockSpec((1, DA_HEAD_DIM), lambda b, h, i: (0, 0))
    meta_spec = pl.BlockSpec((1, 1, META_PAD, LANES), lambda b, h, i: (0, h, 0, 0))
    vmeta_spec = pl.BlockSpec((1, 1, V_ROWS, META_PAD), lambda b, h, i: (0, h, 0, 0))
    return pl.pallas_call(
        _da_attn_kernel,
        grid=(nb, nh, nq),
        in_specs=[
            pl.BlockSpec((1, 1, 1, 2, LANES, tq), lambda b, h, i: (b, h, i, 0, 0, 0)),
            pl.BlockSpec((1, 1, s, LANES), lambda b, h, i: (b, h, 0, 0)),
            pl.BlockSpec((1, 1, nc, V_ROWS, tk), lambda b, h, i: (b, h, 0, 0, 0)),
            meta_spec, vmeta_spec,
            vec_spec, vec_spec, vec_spec, vec_spec,
            pl.BlockSpec((1, DA_V_DIM), lambda b, h, i: (0, 0)),
        ],
        out_specs=pl.BlockSpec((1, tq, DA_V_DIM), lambda b, h, i: (b, i, h)),
        out_shape=jax.ShapeDtypeStruct((nb, s, DA_WIDTH), BF16),
        scratch_shapes=_attn_scratch(cols, tk),
        compiler_params=_cparams(3, _attn_vmem(cols, tk, s, LANES, DA_V_DIM)),
        name="da_attn",
    )(qdat, kda, vdat, kmeta, vmeta_t, *lam_vecs, g)


def _mla_attn(qmt, km, vmt, kmeta, vmeta_t):
    nb, nh, nq, _, tq = qmt.shape
    s = km.shape[2]
    nc, tk = vmt.shape[2], vmt.shape[4]
    return pl.pallas_call(
        _mla_attn_kernel,
        grid=(nb, nh, nq),
        in_specs=[
            pl.BlockSpec((1, 1, 1, MLA_QK_PAD, tq), lambda b, h, i: (b, h, i, 0, 0)),
            pl.BlockSpec((1, 1, s, MLA_QK_PAD), lambda b, h, i: (b, h, 0, 0)),
            pl.BlockSpec((1, 1, nc, V_ROWS, tk), lambda b, h, i: (b, h, 0, 0, 0)),
            pl.BlockSpec((1, 1, META_PAD, MLA_QK_PAD), lambda b, h, i: (0, h, 0, 0)),
            pl.BlockSpec((1, 1, V_ROWS, META_PAD), lambda b, h, i: (0, h, 0, 0)),
        ],
        out_specs=pl.BlockSpec((1, tq, MLA_V), lambda b, h, i: (b, i, h)),
        out_shape=jax.ShapeDtypeStruct((nb, s, MLA_WIDTH), BF16),
        scratch_shapes=_attn_scratch(tq, tk),
        compiler_params=_cparams(3, _attn_vmem(tq, tk, s, MLA_QK_PAD, MLA_V)),
        name="mla_attn",
    )(qmt, km, vmt, kmeta, vmeta_t)


def _outproj_ln_kernel(h_ref, oa_ref, om_ref, wo_ref, g_ref, b_ref, o_ref):
    mix = (jnp.dot(oa_ref[...], wo_ref[0:DA_WIDTH, :], preferred_element_type=F32)
           + jnp.dot(om_ref[...], wo_ref[DA_WIDTH:DA_WIDTH + MLA_WIDTH, :], preferred_element_type=F32))
    o_ref[...] = _layer_norm(ALPHA * h_ref[...] + mix, g_ref[...], b_ref[...])


def _outproj_ln(h, oa, om, wo, g, b, *, tm=512):
    m, d = h.shape
    tm = min(tm, m)
    assert m % tm == 0
    vmem = (2 * 2 * tm * d * 4 + 2 * 2 * tm * DA_WIDTH * 2 + wo.size * 2 + 3 * tm * d * 4) / MIB + 6
    return pl.pallas_call(
        _outproj_ln_kernel,
        grid=(m // tm,),
        in_specs=[
            pl.BlockSpec((tm, d), lambda i: (i, 0)),
            pl.BlockSpec((tm, DA_WIDTH), lambda i: (i, 0)),
            pl.BlockSpec((tm, MLA_WIDTH), lambda i: (i, 0)),
            _resident(wo.shape), _resident(g.shape), _resident(b.shape),
        ],
        out_specs=pl.BlockSpec((tm, d), lambda i: (i, 0)),
        out_shape=jax.ShapeDtypeStruct((m, d), F32),
        compiler_params=_cparams(1, vmem),
        name="outproj_ln",
    )(h, oa, om, wo, g, b)


def _rope_lane_tables(n_pos, rot_dim, period):
    half = rot_dim // 2
    inv = 1.0 / (ROPE_THETA ** (jnp.arange(0, rot_dim, 2, dtype=F32) / rot_dim))
    ang = jnp.arange(n_pos, dtype=F32)[:, None] * inv[None, :]
    cos, sin = jnp.cos(ang), jnp.sin(ang)
    within = jnp.arange(LANES) % period
    idx = within % half
    c = jnp.where(within < rot_dim, cos[:, idx], 1.0)
    sp = jnp.where((within >= half) & (within < rot_dim), sin[:, idx], 0.0)
    sm = jnp.where(within < half, -sin[:, idx], 0.0)
    return c, sp, sm


def _prep_weights(p):
    w = {}
    for name in ("ffn1_w_gate", "ffn1_w_up", "ffn1_w_down", "ffn2_w_gate", "ffn2_w_up", "ffn2_w_down", "w_o", "w_ukv"):
        w[name] = p[name][0].astype(BF16)
    w_in = p["w_in"][0]
    w["w_in"] = jnp.pad(w_in, ((0, 0), (0, LANES - MLA_ROPE))).astype(BF16)
    w_uq = p["w_uq"][0].reshape(MLA_Q_RANK, MLA_HEADS, MLA_NOPE + MLA_ROPE)
    w_uq = jnp.pad(w_uq, ((0, 0), (0, 0), (0, MLA_QK_PAD - MLA_NOPE - MLA_ROPE)))
    w["w_uq"] = w_uq.reshape(MLA_Q_RANK, MLA_HEADS * MLA_QK_PAD).astype(BF16)
    for name in ("ln1_g", "ln1_b", "ln2_g", "ln2_b", "ln3_g", "ln3_b", "da_subln_g", "mla_q_norm_g", "mla_kv_norm_g",
                 "da_lambda_q1", "da_lambda_k1", "da_lambda_q2", "da_lambda_k2"):
        w[name] = p[name].astype(F32)
    return w


def _mixer_inputs(h, nb, s, w, tabs_da, tabs_mla):
    return _inproj(h, nb, s, w["w_in"], w["mla_q_norm_g"], w["w_uq"], w["mla_kv_norm_g"], w["w_ukv"],
                   tabs_da, tabs_mla)


def _trunk(x, meta_kv, w, tabs_da, tabs_mla):
    nb, s, d = x.shape
    h1 = _ffn_ln(x.reshape(nb * s, d), w["ffn1_w_gate"], w["ffn1_w_up"], w["ffn1_w_down"], w["ln1_g"], w["ln1_b"])
    qdat, kda, vdat, qmt, km, vmt = _mixer_inputs(h1, nb, s, w, tabs_da, tabs_mla)
    kda_m, vdat_m, km_m, vmt_m = meta_kv
    lam_vecs = (w["da_lambda_q1"], w["da_lambda_k1"], w["da_lambda_q2"], w["da_lambda_k2"])
    oa = _da_attn(qdat, kda, vdat, kda_m, vdat_m, lam_vecs, w["da_subln_g"])
    om = _mla_attn(qmt, km, vmt, km_m, vmt_m)
    h2 = _outproj_ln(h1, oa.reshape(nb * s, DA_WIDTH), om.reshape(nb * s, MLA_WIDTH), w["w_o"], w["ln2_g"], w["ln2_b"])
    y = _ffn_ln(h2, w["ffn2_w_gate"], w["ffn2_w_up"], w["ffn2_w_down"], w["ln3_g"], w["ln3_b"])
    return y.reshape(nb, s, d)


def kernel(x_prompt, x_sample, meta_tokens, ffn1_w_gate, ffn1_w_up, ffn1_w_down, ln1_g, ln1_b, w_in, da_lambda_q1, da_lambda_k1, da_lambda_q2, da_lambda_k2, da_subln_g, mla_q_norm_g, w_uq, mla_kv_norm_g, w_ukv, w_o, ln2_g, ln2_b, ffn2_w_gate, ffn2_w_up, ffn2_w_down, ln3_g, ln3_b):
    p = dict(ffn1_w_gate=ffn1_w_gate, ffn1_w_up=ffn1_w_up, ffn1_w_down=ffn1_w_down, ln1_g=ln1_g, ln1_b=ln1_b,
             w_in=w_in, da_lambda_q1=da_lambda_q1, da_lambda_k1=da_lambda_k1, da_lambda_q2=da_lambda_q2,
             da_lambda_k2=da_lambda_k2, da_subln_g=da_subln_g, mla_q_norm_g=mla_q_norm_g, w_uq=w_uq,
             mla_kv_norm_g=mla_kv_norm_g, w_ukv=w_ukv, w_o=w_o, ln2_g=ln2_g, ln2_b=ln2_b,
             ffn2_w_gate=ffn2_w_gate, ffn2_w_up=ffn2_w_up, ffn2_w_down=ffn2_w_down, ln3_g=ln3_g, ln3_b=ln3_b)
    w = _prep_weights(p)
    s_max = max(x_prompt.shape[1], x_sample.shape[1])
    tabs_da = _rope_lane_tables(N_META + s_max, DA_ROT, DA_HEAD_DIM)
    tabs_mla = _rope_lane_tables(N_META + s_max, MLA_ROPE, LANES)

    hm = _ffn_ln(meta_tokens.astype(F32), w["ffn1_w_gate"], w["ffn1_w_up"], w["ffn1_w_down"], w["ln1_g"], w["ln1_b"])
    hm = jnp.pad(hm, ((0, PROJ_ROWS - N_META), (0, 0)))
    _, kda_m, vdat_m, _, km_m, vmt_m = _mixer_inputs(
        hm, 1, PROJ_ROWS, w, [t[:PROJ_ROWS] for t in tabs_da], [t[:PROJ_ROWS] for t in tabs_mla])
    meta_kv = (kda_m[:, :, :META_PAD], vdat_m[:, :, 0, :, :META_PAD], km_m[:, :, :META_PAD], vmt_m[:, :, 0, :, :META_PAD])

    seq_da = [t[N_META:] for t in tabs_da]
    seq_mla = [t[N_META:] for t in tabs_mla]
    y_prompt = _trunk(x_prompt, meta_kv, w, seq_da, seq_mla)
    y_sample = _trunk(x_sample, meta_kv, w, seq_da, seq_mla)
    return (y_prompt, y_sample)
```

```python
import functools
import math

import jax
import jax.numpy as jnp
from jax import lax
from jax.experimental import pallas as pl
from jax.experimental.pallas import tpu as pltpu

N_META = 16
ROPE_THETA = 500000.0
DA_HEADS = 8
DA_HEAD_DIM = 64
DA_V_DIM = 2 * DA_HEAD_DIM
DA_ROT = DA_HEAD_DIM // 4
DA_QK_COLS = DA_HEADS * 2 * DA_HEAD_DIM
DA_WIDTH = DA_HEADS * DA_V_DIM
MLA_HEADS = 8
MLA_Q_RANK = 512
MLA_KV_RANK = 256
MLA_NOPE = 128
MLA_ROPE = 64
MLA_V = 128
MLA_WIDTH = MLA_HEADS * MLA_V
MLA_QK_PAD = 256
DEPTH = 1
ALPHA = (2 * DEPTH) ** 0.25
LN_EPS = 1e-5
RMS_EPS = 1e-6
LAMBDA_INIT = 0.8 - 0.6 * math.exp(-0.3 * 0)
LOG2E = 1.4426950408889634

LANES = 128
SUBLANES = 8
META_PAD = 128
NEG_BIG = -1e30

PROJ_ROWS = 256
DA_Q_TILE = 1024
MLA_Q_TILE = 2048
KV_CHUNK = 512
CHUNKS_PER_ITER = 4
ONLINE_CHUNKS_PER_ITER = 2
SCORE_BOUND = 45.0
BOUND_SLACK = 1.001
V_EXTRA_ROWS = 16
V_ROWS = DA_V_DIM + V_EXTRA_ROWS

F32 = jnp.float32
BF16 = jnp.bfloat16
MIB = 1024 * 1024


def _cparams(n_axes, vmem_mib):
    return pltpu.CompilerParams(
        dimension_semantics=("arbitrary",) * n_axes,
        vmem_limit_bytes=int(vmem_mib * MIB),
    )


def _resident(shape):
    n = len(shape)
    return pl.BlockSpec(shape, lambda *_: (0,) * n, pipeline_mode=pl.Buffered(1))


def _layer_norm(y, g, b):
    mu = jnp.mean(y, axis=-1, keepdims=True)
    d = y - mu
    var = jnp.mean(d * d, axis=-1, keepdims=True)
    return d * lax.rsqrt(var + LN_EPS) * g + b


def _rms_norm(x, g):
    return x * lax.rsqrt(jnp.mean(x * x, axis=-1, keepdims=True) + RMS_EPS) * g


def _ffn_ln_kernel(x_ref, wg_ref, wu_ref, wd_ref, g_ref, b_ref, o_ref, xb_ref, acc_ref):
    j = pl.program_id(1)

    @pl.when(j == 0)
    def _():
        xb_ref[...] = x_ref[...].astype(BF16)
        acc_ref[...] = jnp.zeros_like(acc_ref)

    xb = xb_ref[...]
    gate = jnp.dot(xb, wg_ref[...], preferred_element_type=F32)
    up = jnp.dot(xb, wu_ref[...], preferred_element_type=F32)
    hid = (gate * (1.0 / (1.0 + jnp.exp(-gate)))) * up
    acc_ref[...] += jnp.dot(hid.astype(BF16), wd_ref[...], preferred_element_type=F32)

    @pl.when(j == pl.num_programs(1) - 1)
    def _():
        y = ALPHA * x_ref[...] + acc_ref[...]
        o_ref[...] = _layer_norm(y, g_ref[...], b_ref[...])


def _ffn_ln(x, wg, wu, wd, g, b, *, tm=512, tf=512):
    m, d = x.shape
    f = wg.shape[1]
    tm = min(tm, m)
    tf = min(tf, f)
    assert m % tm == 0 and f % tf == 0
    vmem = (2 * 2 * tm * d * 4 + 2 * 3 * d * tf * 2 + tm * d * 6 + 4 * tm * tf * 4) / MIB + 6
    return pl.pallas_call(
        _ffn_ln_kernel,
        grid=(m // tm, f // tf),
        in_specs=[
            pl.BlockSpec((tm, d), lambda i, j: (i, 0)),
            pl.BlockSpec((d, tf), lambda i, j: (0, j)),
            pl.BlockSpec((d, tf), lambda i, j: (0, j)),
            pl.BlockSpec((tf, d), lambda i, j: (j, 0)),
            pl.BlockSpec((1, d), lambda i, j: (0, 0)),
            pl.BlockSpec((1, d), lambda i, j: (0, 0)),
        ],
        out_specs=pl.BlockSpec((tm, d), lambda i, j: (i, 0)),
        out_shape=jax.ShapeDtypeStruct((m, d), F32),
        scratch_shapes=[pltpu.VMEM((tm, d), BF16), pltpu.VMEM((tm, d), F32)],
        compiler_params=_cparams(2, vmem),
        name="ffn_ln",
    )(x, wg, wu, wd, g, b)


def _rope(x, c, sp, sm, half):
    return x * c + pltpu.roll(x, half, 1) * sp + pltpu.roll(x, LANES - half, 1) * sm


def _inproj_kernel(h_ref, win_ref, gq_ref, wuq_ref, gkv_ref, wukv_ref,
                   cda_ref, spda_ref, smda_ref, cm_ref, spm_ref, smm_ref,
                   qdat_ref, kda_ref, vdat_ref, qmt_ref, km_ref, vmt_ref):
    tm = h_ref.shape[0]
    hb = h_ref[...].astype(BF16)

    def proj(lo, hi):
        return jnp.dot(hb, win_ref[:, lo:hi], preferred_element_type=F32)

    cda, spda, smda = cda_ref[...], spda_ref[...], smda_ref[...]
    cm, spm, smm = cm_ref[...], spm_ref[...], smm_ref[...]
    da_half = DA_ROT // 2
    mla_half = MLA_ROPE // 2
    map1_dims = lax.broadcasted_iota(jnp.int32, (LANES, tm), 0) < DA_HEAD_DIM
    da_scale = DA_HEAD_DIM ** -0.5 * LOG2E
    mla_scale = (MLA_NOPE + MLA_ROPE) ** -0.5 * LOG2E
    sum_rows = (lax.broadcasted_iota(jnp.int32, (V_EXTRA_ROWS, tm), 0) == 0).astype(BF16)

    c0 = 0
    zq = proj(c0, c0 + DA_QK_COLS)
    for h in range(DA_HEADS):
        rt = (_rope(zq[:, h * LANES:(h + 1) * LANES], cda, spda, smda, da_half) * da_scale).T
        qdat_ref[0, h, 0, 0] = jnp.where(map1_dims, rt, 0.0).astype(BF16)
        qdat_ref[0, h, 0, 1] = jnp.where(map1_dims, 0.0, rt).astype(BF16)
    c0 += DA_QK_COLS
    zk = proj(c0, c0 + DA_QK_COLS)
    for h in range(DA_HEADS):
        kda_ref[0, h] = _rope(zk[:, h * LANES:(h + 1) * LANES], cda, spda, smda, da_half).astype(BF16)
    c0 += DA_QK_COLS
    zv = proj(c0, c0 + DA_WIDTH)
    for h in range(DA_HEADS):
        vdat_ref[0, h, 0, 0:DA_V_DIM, :] = zv[:, h * DA_V_DIM:(h + 1) * DA_V_DIM].T.astype(BF16)
        vdat_ref[0, h, 0, DA_V_DIM:V_ROWS, :] = sum_rows
    c0 += DA_WIDTH

    cq = _rms_norm(proj(c0, c0 + MLA_Q_RANK), gq_ref[...]).astype(BF16)
    c0 += MLA_Q_RANK
    qm = jnp.dot(cq, wuq_ref[...], preferred_element_type=F32) * mla_scale
    for h in range(MLA_HEADS):
        base = h * MLA_QK_PAD
        qmt_ref[0, h, 0, 0:MLA_NOPE, :] = qm[:, base:base + MLA_NOPE].T.astype(BF16)
        qmt_ref[0, h, 0, MLA_NOPE:MLA_QK_PAD, :] = _rope(
            qm[:, base + MLA_NOPE:base + MLA_QK_PAD], cm, spm, smm, mla_half).T.astype(BF16)

    ckv = _rms_norm(proj(c0, c0 + MLA_KV_RANK), gkv_ref[...]).astype(BF16)
    c0 += MLA_KV_RANK
    kv = jnp.dot(ckv, wukv_ref[...], preferred_element_type=F32)
    kr = _rope(proj(c0, c0 + LANES), cm, spm, smm, mla_half).astype(BF16)
    for h in range(MLA_HEADS):
        base = h * (MLA_NOPE + MLA_V)
        km_ref[0, h, :, 0:MLA_NOPE] = kv[:, base:base + MLA_NOPE].astype(BF16)
        km_ref[0, h, :, MLA_NOPE:MLA_QK_PAD] = kr
        vmt_ref[0, h, 0, 0:MLA_V, :] = kv[:, base + MLA_NOPE:base + MLA_NOPE + MLA_V].T.astype(BF16)
        vmt_ref[0, h, 0, MLA_V:V_ROWS, :] = sum_rows


def _inproj(h, nb, s, win, gq, wuq, gkv, wukv, tabs_da, tabs_mla):
    m, d = h.shape
    tm = PROJ_ROWS
    ck = min(KV_CHUNK, s)
    tqd = min(DA_Q_TILE, s)
    tqm = min(MLA_Q_TILE, s)
    assert m == nb * s and all(t % tm == 0 and s % t == 0 for t in (tm, ck, tqd, tqm))
    spb = s // tm
    tab_spec = pl.BlockSpec((tm, LANES), lambda i: (i % spb, 0))

    def rows_spec(width):
        return pl.BlockSpec((1, DA_HEADS, tm, width), lambda i: (i // spb, 0, i % spb, 0))

    def cols_spec(rows, tile, *mid):
        per = tile // tm
        zeros = (0,) * (len(mid) + 1)
        return pl.BlockSpec((1, DA_HEADS, 1) + mid + (rows, tm),
                            lambda i: (i // spb, 0, (i % spb) // per) + zeros + ((i % spb) % per,))

    def shape(*tail):
        return jax.ShapeDtypeStruct((nb, DA_HEADS) + tail, BF16)

    vmem = (2 * tm * d * 4 + (win.size + wuq.size + wukv.size) * 2 + 2 * 6 * tm * LANES * 4
            + 2 * tm * 8 * (256 + 128 + 128 + 256 + 256 + 128) * 2 + 8 * tm * 2048 * 4) / MIB + 6
    return pl.pallas_call(
        _inproj_kernel,
        grid=(m // tm,),
        in_specs=[
            pl.BlockSpec((tm, d), lambda i: (i, 0)),
            _resident(win.shape), _resident(gq.shape), _resident(wuq.shape),
            _resident(gkv.shape), _resident(wukv.shape),
            tab_spec, tab_spec, tab_spec, tab_spec, tab_spec, tab_spec,
        ],
        out_specs=[
            cols_spec(LANES, tqd, 2),
            rows_spec(LANES), cols_spec(V_ROWS, ck),
            cols_spec(MLA_QK_PAD, tqm), rows_spec(MLA_QK_PAD), cols_spec(V_ROWS, ck),
        ],
        out_shape=[
            shape(s // tqd, 2, LANES, tqd), shape(s, LANES), shape(s // ck, V_ROWS, ck),
            shape(s // tqm, MLA_QK_PAD, tqm), shape(s, MLA_QK_PAD), shape(s // ck, V_ROWS, ck),
        ],
        compiler_params=_cparams(1, vmem),
        name="inproj",
    )(h, win, gq, wuq, gkv, wukv, *tabs_da, *tabs_mla)


def _masked_meta_scores(kmeta, qt):
    s = jnp.dot(kmeta, qt, preferred_element_type=F32)
    return jnp.where(lax.broadcasted_iota(jnp.int32, s.shape, 0) < N_META, s, NEG_BIG)


def _max_key_norm(k_ref, kmeta, kmax_scr):
    s_len, dk = k_ref.shape[2], k_ref.shape[3]
    rows = min(KV_CHUNK, s_len)

    def max_sq_norm(x):
        xf = x.astype(F32)
        sq = xf * xf
        part = sq[:, 0:LANES]
        for t in range(1, dk // LANES):
            part = part + sq[:, t * LANES:(t + 1) * LANES]
        return jnp.max(jnp.sum(part, axis=1, keepdims=True), axis=0, keepdims=True)

    def body(c, best):
        off = pl.multiple_of(c * rows, rows)
        return jnp.maximum(best, max_sq_norm(k_ref[0, 0, pl.ds(off, rows), :]))

    best = lax.fori_loop(0, s_len // rows, body, max_sq_norm(kmeta))
    kmax_scr[...] = jnp.broadcast_to(jnp.sqrt(best), kmax_scr.shape)


def _score_shift(qt, kmax_scr):
    qf = qt.astype(F32)
    return jnp.sqrt(jnp.sum(qf * qf, axis=0, keepdims=True)) * (kmax_scr[0:1, 0:1] * BOUND_SLACK)


def _flash_cols_shifted(qt, shift, k_ref, vt_ref, kmeta, vmeta_t, acc_scr):
    n_chunks, tk = vt_ref.shape[2], vt_ref.shape[4]
    group = CHUNKS_PER_ITER

    def weights(s):
        return jnp.exp2(s - shift).astype(BF16)

    def chunk_pv(c):
        off = c * tk if isinstance(c, int) else pl.multiple_of(c * tk, tk)
        s = jnp.dot(k_ref[0, 0, pl.ds(off, tk), :], qt, preferred_element_type=F32)
        return jnp.dot(vt_ref[0, 0, c], weights(s), preferred_element_type=F32)

    acc_scr[...] = jnp.dot(vmeta_t, weights(_masked_meta_scores(kmeta, qt)), preferred_element_type=F32)

    def body(j, carry):
        total = chunk_pv(group * j)
        for u in range(1, group):
            total = total + chunk_pv(group * j + u)
        acc_scr[...] += total
        return carry

    n_iter = n_chunks // group if n_chunks >= 2 * group else 0
    lax.fori_loop(0, n_iter, body, 0)
    for c in range(group * n_iter, n_chunks):
        acc_scr[...] += chunk_pv(c)


def _flash_cols_online(qt, k_ref, vt_ref, kmeta, vmeta_t, s_scrs, cmax_scrs, m_scr, acc_scr):
    n_chunks, tk = vt_ref.shape[2], vt_ref.shape[4]
    r = qt.shape[1]
    group = ONLINE_CHUNKS_PER_ITER
    assert group % 2 == 0

    def bcast(x):
        return jnp.broadcast_to(x, (SUBLANES, r))

    def scores(c):
        off = c * tk if isinstance(c, int) else pl.multiple_of(c * tk, tk)
        return jnp.dot(k_ref[0, 0, pl.ds(off, tk), :], qt, preferred_element_type=F32)

    def put(s, slot, rows):
        s_scrs[slot][0:rows, :] = s
        cmax_scrs[slot][...] = bcast(jnp.max(s, axis=0, keepdims=True))

    def fold(slot, rows, vt, first=False):
        cmax = cmax_scrs[slot][0:1, :]
        m_new = cmax if first else jnp.maximum(m_scr[0:1, :], cmax)
        p = jnp.exp2(s_scrs[slot][0:rows, :] - m_new)
        pv = jnp.dot(vt, p.astype(BF16), preferred_element_type=F32)
        if first:
            acc_scr[...] = pv
        else:
            acc_scr[...] = jnp.exp2(m_scr[0:1, :] - m_new) * acc_scr[...] + pv
        m_scr[...] = bcast(m_new)

    put(_masked_meta_scores(kmeta, qt), 1, META_PAD)
    put(scores(0), 0, tk)
    fold(1, META_PAD, vmeta_t, first=True)

    def step(c, parity, last):
        if not last:
            put(scores(c + 1), 1 - parity, tk)
        fold(parity, tk, vt_ref[0, 0, c])

    n_iter = (n_chunks - 1) // group

    def body(j, carry):
        for u in range(group):
            step(group * j + u, u % 2, False)
        return carry

    lax.fori_loop(0, n_iter, body, 0)
    for c in range(group * n_iter, n_chunks):
        step(c, c % 2, c == n_chunks - 1)


def _normalised(acc_scr, dv):
    return acc_scr[0:dv, :] * (1.0 / acc_scr[dv:dv + 1, :])


def _flash_cols(qt, k_ref, vt_ref, kmeta, vmeta_t, s0_scr, s1_scr, c0_scr, c1_scr, m_scr, kmax_scr, acc_scr):
    @pl.when(pl.program_id(2) == 0)
    def _():
        _max_key_norm(k_ref, kmeta, kmax_scr)

    shift = _score_shift(qt, kmax_scr)
    bounded = jnp.max(shift) <= SCORE_BOUND

    @pl.when(bounded)
    def _():
        _flash_cols_shifted(qt, shift, k_ref, vt_ref, kmeta, vmeta_t, acc_scr)

    @pl.when(jnp.logical_not(bounded))
    def _():
        _flash_cols_online(qt, k_ref, vt_ref, kmeta, vmeta_t, (s0_scr, s1_scr), (c0_scr, c1_scr), m_scr, acc_scr)


def _da_attn_kernel(qt_ref, k_ref, vt_ref, kmeta_ref, vmeta_ref, lq1_ref, lk1_ref, lq2_ref, lk2_ref,
                    g_ref, o_ref, *scratch):
    tq = o_ref.shape[1]
    qt = jnp.concatenate([qt_ref[0, 0, 0, 0], qt_ref[0, 0, 0, 1]], axis=1)
    _flash_cols(qt, k_ref, vt_ref, kmeta_ref[0, 0], vmeta_ref[0, 0], *scratch)
    acc_scr = scratch[-1]
    at = _normalised(acc_scr, DA_V_DIM)
    lam = (jnp.exp(jnp.sum(lq1_ref[...] * lk1_ref[...], axis=1, keepdims=True))
           - jnp.exp(jnp.sum(lq2_ref[...] * lk2_ref[...], axis=1, keepdims=True)) + LAMBDA_INIT)
    ot = at[:, :tq] - lam * at[:, tq:]
    inv = lax.rsqrt(jnp.mean(ot * ot, axis=0, keepdims=True) + RMS_EPS)
    o_ref[0] = ((ot * inv).T * g_ref[...] * (1.0 - LAMBDA_INIT)).astype(BF16)


def _mla_attn_kernel(qt_ref, k_ref, vt_ref, kmeta_ref, vmeta_ref, o_ref, *scratch):
    _flash_cols(qt_ref[0, 0, 0], k_ref, vt_ref, kmeta_ref[0, 0], vmeta_ref[0, 0], *scratch)
    o_ref[0] = _normalised(scratch[-1], MLA_V).T.astype(BF16)


def _attn_scratch(cols, tk):
    stat = pltpu.VMEM((SUBLANES, cols), F32)
    return [pltpu.VMEM((tk, cols), F32), pltpu.VMEM((tk, cols), F32), stat, stat, stat,
            pltpu.VMEM((SUBLANES, LANES), F32), pltpu.VMEM((V_ROWS, cols), F32)]


def _attn_vmem(cols, tk, s, dk, dv):
    return (2 * s * (dk + dv) * 2 + 2 * dk * cols * 2 + 2 * tk * cols * 4 + dv * cols * 4
            + 4 * tk * cols * 4 + 2 * cols * dv * 2) / MIB + 8


def _da_attn(qdat, kda, vdat, kmeta, vmeta_t, lam_vecs, g):
    nb, nh, nq, _, _, tq = qdat.shape
    cols = 2 * tq
    s = kda.shape[2]
    nc, tk = vdat.shape[2], vdat.shape[4]
    vec_spec = pl.BlockSpec((1, DA_HEAD_DIM), lambda b, h, i: (0, 0))
    meta_spec = pl.BlockSpec((1, 1, META_PAD, LANES), lambda b, h, i: (0, h, 0, 0))
    vmeta_spec = pl.BlockSpec((1, 1, V_ROWS, META_PAD), lambda b, h, i: (0, h, 0, 0))
    return pl.pallas_call(
        _da_attn_kernel,
        grid=(nb, nh, nq),
        in_specs=[
            pl.BlockSpec((1, 1, 1, 2, LANES, tq), lambda b, h, i: (b, h, i, 0, 0, 0)),
            pl.BlockSpec((1, 1, s, LANES), lambda b, h, i: (b, h, 0, 0)),
            pl.BlockSpec((1, 1, nc, V_ROWS, tk), lambda b, h, i: (b, h, 0, 0, 0)),
            meta_spec, vmeta_spec,
            vec_spec, vec_spec, vec_spec, vec_spec,
            pl.BlockSpec((1, DA_V_DIM), lambda b, h, i: (0, 0)),
        ],
        out_specs=pl.BlockSpec((1, tq, DA_V_DIM), lambda b, h, i: (b, i, h)),
        out_shape=jax.ShapeDtypeStruct((nb, s, DA_WIDTH), BF16),
        scratch_shapes=_attn_scratch(cols, tk),
        compiler_params=_cparams(3, _attn_vmem(cols, tk, s, LANES, DA_V_DIM)),
        name="da_attn",
    )(qdat, kda, vdat, kmeta, vmeta_t, *lam_vecs, g)


def _mla_attn(qmt, km, vmt, kmeta, vmeta_t):
    nb, nh, nq, _, tq = qmt.shape
    s = km.shape[2]
    nc, tk = vmt.shape[2], vmt.shape[4]
    return pl.pallas_call(
        _mla_attn_kernel,
        grid=(nb, nh, nq),
        in_specs=[
            pl.BlockSpec((1, 1, 1, MLA_QK_PAD, tq), lambda b, h, i: (b, h, i, 0, 0)),
            pl.BlockSpec((1, 1, s, MLA_QK_PAD), lambda b, h, i: (b, h, 0, 0)),
            pl.BlockSpec((1, 1, nc, V_ROWS, tk), lambda b, h, i: (b, h, 0, 0, 0)),
            pl.BlockSpec((1, 1, META_PAD, MLA_QK_PAD), lambda b, h, i: (0, h, 0, 0)),
            pl.BlockSpec((1, 1, V_ROWS, META_PAD), lambda b, h, i: (0, h, 0, 0)),
        ],
        out_specs=pl.BlockSpec((1, tq, MLA_V), lambda b, h, i: (b, i, h)),
        out_shape=jax.ShapeDtypeStruct((nb, s, MLA_WIDTH), BF16),
        scratch_shapes=_attn_scratch(tq, tk),
        compiler_params=_cparams(3, _attn_vmem(tq, tk, s, MLA_QK_PAD, MLA_V)),
        name="mla_attn",
    )(qmt, km, vmt, kmeta, vmeta_t)


def _outproj_ln_kernel(h_ref, oa_ref, om_ref, wo_ref, g_ref, b_ref, o_ref):
    mix = (jnp.dot(oa_ref[...], wo_ref[0:DA_WIDTH, :], preferred_element_type=F32)
           + jnp.dot(om_ref[...], wo_ref[DA_WIDTH:DA_WIDTH + MLA_WIDTH, :], preferred_element_type=F32))
    o_ref[...] = _layer_norm(ALPHA * h_ref[...] + mix, g_ref[...], b_ref[...])


def _outproj_ln(h, oa, om, wo, g, b, *, tm=512):
    m, d = h.shape
    tm = min(tm, m)
    assert m % tm == 0
    vmem = (2 * 2 * tm * d * 4 + 2 * 2 * tm * DA_WIDTH * 2 + wo.size * 2 + 3 * tm * d * 4) / MIB + 6
    return pl.pallas_call(
        _outproj_ln_kernel,
        grid=(m // tm,),
        in_specs=[
            pl.BlockSpec((tm, d), lambda i: (i, 0)),
            pl.BlockSpec((tm, DA_WIDTH), lambda i: (i, 0)),
            pl.BlockSpec((tm, MLA_WIDTH), lambda i: (i, 0)),
            _resident(wo.shape), _resident(g.shape), _resident(b.shape),
        ],
        out_specs=pl.BlockSpec((tm, d), lambda i: (i, 0)),
        out_shape=jax.ShapeDtypeStruct((m, d), F32),
        compiler_params=_cparams(1, vmem),
        name="outproj_ln",
    )(h, oa, om, wo, g, b)


def _rope_lane_tables(n_pos, rot_dim, period):
    half = rot_dim // 2
    inv = 1.0 / (ROPE_THETA ** (jnp.arange(0, rot_dim, 2, dtype=F32) / rot_dim))
    ang = jnp.arange(n_pos, dtype=F32)[:, None] * inv[None, :]
    cos, sin = jnp.cos(ang), jnp.sin(ang)
    within = jnp.arange(LANES) % period
    idx = within % half
    c = jnp.where(within < rot_dim, cos[:, idx], 1.0)
    sp = jnp.where((within >= half) & (within < rot_dim), sin[:, idx], 0.0)
    sm = jnp.where(within < half, -sin[:, idx], 0.0)
    return c, sp, sm


def _prep_weights(p):
    w = {}
    for name in ("ffn1_w_gate", "ffn1_w_up", "ffn2_w_gate", "ffn2_w_up", "w_o", "w_ukv"):
        w[name] = p[name][0].astype(BF16)
    for name in ("ffn1_w_down", "ffn2_w_down"):
        w[name] = (p[name][0] * 0.5).astype(BF16)
    w_in = p["w_in"][0]
    w["w_in"] = jnp.pad(w_in, ((0, 0), (0, LANES - MLA_ROPE))).astype(BF16)
    w_uq = p["w_uq"][0].reshape(MLA_Q_RANK, MLA_HEADS, MLA_NOPE + MLA_ROPE)
    w_uq = jnp.pad(w_uq, ((0, 0), (0, 0), (0, MLA_QK_PAD - MLA_NOPE - MLA_ROPE)))
    w["w_uq"] = w_uq.reshape(MLA_Q_RANK, MLA_HEADS * MLA_QK_PAD).astype(BF16)
    for name in ("ln1_g", "ln1_b", "ln2_g", "ln2_b", "ln3_g", "ln3_b", "da_subln_g", "mla_q_norm_g", "mla_kv_norm_g",
                 "da_lambda_q1", "da_lambda_k1", "da_lambda_q2", "da_lambda_k2"):
        w[name] = p[name].astype(F32)
    return w


def _mixer_inputs(h, nb, s, w, tabs_da, tabs_mla):
    return _inproj(h, nb, s, w["w_in"], w["mla_q_norm_g"], w["w_uq"], w["mla_kv_norm_g"], w["w_ukv"],
                   tabs_da, tabs_mla)


def _trunk(x, meta_kv, w, tabs_da, tabs_mla):
    nb, s, d = x.shape
    h1 = _ffn_ln(x.reshape(nb * s, d), w["ffn1_w_gate"], w["ffn1_w_up"], w["ffn1_w_down"], w["ln1_g"], w["ln1_b"])
    qdat, kda, vdat, qmt, km, vmt = _mixer_inputs(h1, nb, s, w, tabs_da, tabs_mla)
    kda_m, vdat_m, km_m, vmt_m = meta_kv
    lam_vecs = (w["da_lambda_q1"], w["da_lambda_k1"], w["da_lambda_q2"], w["da_lambda_k2"])
    oa = _da_attn(qdat, kda, vdat, kda_m, vdat_m, lam_vecs, w["da_subln_g"])
    om = _mla_attn(qmt, km, vmt, km_m, vmt_m)
    h2 = _outproj_ln(h1, oa.reshape(nb * s, DA_WIDTH), om.reshape(nb * s, MLA_WIDTH), w["w_o"], w["ln2_g"], w["ln2_b"])
    y = _ffn_ln(h2, w["ffn2_w_gate"], w["ffn2_w_up"], w["ffn2_w_down"], w["ln3_g"], w["ln3_b"])
    return y.reshape(nb, s, d)


def kernel(x_prompt, x_sample, meta_tokens, ffn1_w_gate, ffn1_w_up, ffn1_w_down, ln1_g, ln1_b, w_in, da_lambda_q1, da_lambda_k1, da_lambda_q2, da_lambda_k2, da_subln_g, mla_q_norm_g, w_uq, mla_kv_norm_g, w_ukv, w_o, ln2_g, ln2_b, ffn2_w_gate, ffn2_w_up, ffn2_w_down, ln3_g, ln3_b):
    p = dict(ffn1_w_gate=ffn1_w_gate, ffn1_w_up=ffn1_w_up, ffn1_w_down=ffn1_w_down, ln1_g=ln1_g, ln1_b=ln1_b,
             w_in=w_in, da_lambda_q1=da_lambda_q1, da_lambda_k1=da_lambda_k1, da_lambda_q2=da_lambda_q2,
             da_lambda_k2=da_lambda_k2, da_subln_g=da_subln_g, mla_q_norm_g=mla_q_norm_g, w_uq=w_uq,
             mla_kv_norm_g=mla_kv_norm_g, w_ukv=w_ukv, w_o=w_o, ln2_g=ln2_g, ln2_b=ln2_b,
             ffn2_w_gate=ffn2_w_gate, ffn2_w_up=ffn2_w_up, ffn2_w_down=ffn2_w_down, ln3_g=ln3_g, ln3_b=ln3_b)
    w = _prep_weights(p)
    s_max = max(x_prompt.shape[1], x_sample.shape[1])
    tabs_da = _rope_lane_tables(N_META + s_max, DA_ROT, DA_HEAD_DIM)
    tabs_mla = _rope_lane_tables(N_META + s_max, MLA_ROPE, LANES)

    hm = _ffn_ln(meta_tokens.astype(F32), w["ffn1_w_gate"], w["ffn1_w_up"], w["ffn1_w_down"], w["ln1_g"], w["ln1_b"])
    hm = jnp.pad(hm, ((0, PROJ_ROWS - N_META), (0, 0)))
    _, kda_m, vdat_m, _, km_m, vmt_m = _mixer_inputs(
        hm, 1, PROJ_ROWS, w, [t[:PROJ_ROWS] for t in tabs_da], [t[:PROJ_ROWS] for t in tabs_mla])
    meta_kv = (kda_m[:, :, :META_PAD], vdat_m[:, :, 0, :, :META_PAD], km_m[:, :, :META_PAD], vmt_m[:, :, 0, :, :META_PAD])

    seq_da = [t[N_META:] for t in tabs_da]
    seq_mla = [t[N_META:] for t in tabs_mla]
    y_prompt = _trunk(x_prompt, meta_kv, w, seq_da, seq_mla)
    y_sample = _trunk(x_sample, meta_kv, w, seq_da, seq_mla)
    return (y_prompt, y_sample)
```

```python
import functools
import math

import jax
import jax.numpy as jnp
from jax import lax
from jax.experimental import pallas as pl
from jax.experimental.pallas import tpu as pltpu

N_META = 16
ROPE_THETA = 500000.0
DA_HEADS = 8
DA_HEAD_DIM = 64
DA_V_DIM = 2 * DA_HEAD_DIM
DA_ROT = DA_HEAD_DIM // 4
DA_QK_COLS = DA_HEADS * 2 * DA_HEAD_DIM
DA_WIDTH = DA_HEADS * DA_V_DIM
MLA_HEADS = 8
MLA_Q_RANK = 512
MLA_KV_RANK = 256
MLA_NOPE = 128
MLA_ROPE = 64
MLA_V = 128
MLA_WIDTH = MLA_HEADS * MLA_V
MLA_QK_PAD = 256
DEPTH = 1
ALPHA = (2 * DEPTH) ** 0.25
LN_EPS = 1e-5
RMS_EPS = 1e-6
LAMBDA_INIT = 0.8 - 0.6 * math.exp(-0.3 * 0)
LOG2E = 1.4426950408889634

LANES = 128
SUBLANES = 8
META_PAD = 128
NEG_BIG = -1e30

PROJ_ROWS = 256
DA_Q_TILE = 1024
MLA_Q_TILE = 2048
KV_CHUNK = 512
CHUNKS_PER_ITER = 8
ONLINE_CHUNKS_PER_ITER = 2
OUTPROJ_SUBBLOCKS = 2
SCORE_BOUND = 45.0
BOUND_SLACK = 1.001
V_EXTRA_ROWS = 16
V_ROWS = DA_V_DIM + V_EXTRA_ROWS

F32 = jnp.float32
BF16 = jnp.bfloat16
MIB = 1024 * 1024


def _cparams(n_axes, vmem_mib):
    return pltpu.CompilerParams(
        dimension_semantics=("arbitrary",) * n_axes,
        vmem_limit_bytes=int(vmem_mib * MIB),
    )


def _resident(shape):
    n = len(shape)
    return pl.BlockSpec(shape, lambda *_: (0,) * n, pipeline_mode=pl.Buffered(1))


def _layer_norm(y, g, b):
    mu = jnp.mean(y, axis=-1, keepdims=True)
    d = y - mu
    var = jnp.mean(d * d, axis=-1, keepdims=True)
    return d * lax.rsqrt(var + LN_EPS) * g + b


def _rms_norm(x, g):
    return x * lax.rsqrt(jnp.mean(x * x, axis=-1, keepdims=True) + RMS_EPS) * g


def _ffn_ln_kernel(x_ref, wg_ref, wu_ref, wd_ref, g_ref, b_ref, o_ref, xb_ref, acc_ref):
    j = pl.program_id(1)

    @pl.when(j == 0)
    def _():
        xb_ref[...] = x_ref[...].astype(BF16)
        acc_ref[...] = jnp.zeros_like(acc_ref)

    xb = xb_ref[...]
    gate = jnp.dot(xb, wg_ref[...], preferred_element_type=F32)
    up = jnp.dot(xb, wu_ref[...], preferred_element_type=F32)
    hid = (gate * (1.0 / (1.0 + jnp.exp(-gate)))) * up
    acc_ref[...] += jnp.dot(hid.astype(BF16), wd_ref[...], preferred_element_type=F32)

    @pl.when(j == pl.num_programs(1) - 1)
    def _():
        y = ALPHA * x_ref[...] + acc_ref[...]
        o_ref[...] = _layer_norm(y, g_ref[...], b_ref[...])


def _ffn_ln(x, wg, wu, wd, g, b, *, tm=512, tf=512):
    m, d = x.shape
    f = wg.shape[1]
    tm = min(tm, m)
    tf = min(tf, f)
    assert m % tm == 0 and f % tf == 0
    vmem = (2 * 2 * tm * d * 4 + 2 * 3 * d * tf * 2 + tm * d * 6 + 4 * tm * tf * 4) / MIB + 6
    return pl.pallas_call(
        _ffn_ln_kernel,
        grid=(m // tm, f // tf),
        in_specs=[
            pl.BlockSpec((tm, d), lambda i, j: (i, 0)),
            pl.BlockSpec((d, tf), lambda i, j: (0, j)),
            pl.BlockSpec((d, tf), lambda i, j: (0, j)),
            pl.BlockSpec((tf, d), lambda i, j: (j, 0)),
            pl.BlockSpec((1, d), lambda i, j: (0, 0)),
            pl.BlockSpec((1, d), lambda i, j: (0, 0)),
        ],
        out_specs=pl.BlockSpec((tm, d), lambda i, j: (i, 0)),
        out_shape=jax.ShapeDtypeStruct((m, d), F32),
        scratch_shapes=[pltpu.VMEM((tm, d), BF16), pltpu.VMEM((tm, d), F32)],
        compiler_params=_cparams(2, vmem),
        name="ffn_ln",
    )(x, wg, wu, wd, g, b)


def _rope(x, c, sp, sm, half):
    return x * c + pltpu.roll(x, half, 1) * sp + pltpu.roll(x, LANES - half, 1) * sm


def _inproj_kernel(h_ref, win_ref, gq_ref, wuq_ref, gkv_ref, wukv_ref,
                   cda_ref, spda_ref, smda_ref, cm_ref, spm_ref, smm_ref,
                   qdat_ref, kda_ref, vdat_ref, qmt_ref, km_ref, vmt_ref):
    tm = h_ref.shape[0]
    hb = h_ref[...].astype(BF16)

    def proj(lo, hi):
        return jnp.dot(hb, win_ref[:, lo:hi], preferred_element_type=F32)

    cda, spda, smda = cda_ref[...], spda_ref[...], smda_ref[...]
    cm, spm, smm = cm_ref[...], spm_ref[...], smm_ref[...]
    da_half = DA_ROT // 2
    mla_half = MLA_ROPE // 2
    map1_dims = lax.broadcasted_iota(jnp.int32, (LANES, tm), 0) < DA_HEAD_DIM
    da_scale = DA_HEAD_DIM ** -0.5 * LOG2E
    mla_scale = (MLA_NOPE + MLA_ROPE) ** -0.5 * LOG2E
    sum_rows = (lax.broadcasted_iota(jnp.int32, (V_EXTRA_ROWS, tm), 0) == 0).astype(BF16)

    c0 = 2 * DA_QK_COLS + DA_WIDTH
    cq = _rms_norm(proj(c0, c0 + MLA_Q_RANK), gq_ref[...]).astype(BF16)
    c0 += MLA_Q_RANK
    qm = jnp.dot(cq, wuq_ref[...], preferred_element_type=F32) * mla_scale
    for h in range(MLA_HEADS):
        base = h * MLA_QK_PAD
        qmt_ref[0, h, 0, 0:MLA_NOPE, :] = qm[:, base:base + MLA_NOPE].T.astype(BF16)
        qmt_ref[0, h, 0, MLA_NOPE:MLA_QK_PAD, :] = _rope(
            qm[:, base + MLA_NOPE:base + MLA_QK_PAD], cm, spm, smm, mla_half).T.astype(BF16)

    ckv = _rms_norm(proj(c0, c0 + MLA_KV_RANK), gkv_ref[...]).astype(BF16)
    c0 += MLA_KV_RANK
    kv = jnp.dot(ckv, wukv_ref[...], preferred_element_type=F32)
    kr = _rope(proj(c0, c0 + LANES), cm, spm, smm, mla_half).astype(BF16)
    for h in range(MLA_HEADS):
        base = h * (MLA_NOPE + MLA_V)
        km_ref[0, h, :, 0:MLA_NOPE] = kv[:, base:base + MLA_NOPE].astype(BF16)
        km_ref[0, h, :, MLA_NOPE:MLA_QK_PAD] = kr
        vmt_ref[0, h, 0, 0:MLA_V, :] = kv[:, base + MLA_NOPE:base + MLA_NOPE + MLA_V].T.astype(BF16)
        vmt_ref[0, h, 0, MLA_V:V_ROWS, :] = sum_rows

    c0 = 0
    zq = proj(c0, c0 + DA_QK_COLS)
    for h in range(DA_HEADS):
        rt = (_rope(zq[:, h * LANES:(h + 1) * LANES], cda, spda, smda, da_half) * da_scale).T
        qdat_ref[0, h, 0, 0] = jnp.where(map1_dims, rt, 0.0).astype(BF16)
        qdat_ref[0, h, 0, 1] = jnp.where(map1_dims, 0.0, rt).astype(BF16)
    c0 += DA_QK_COLS
    zk = proj(c0, c0 + DA_QK_COLS)
    for h in range(DA_HEADS):
        kda_ref[0, h] = _rope(zk[:, h * LANES:(h + 1) * LANES], cda, spda, smda, da_half).astype(BF16)
    c0 += DA_QK_COLS
    zv = proj(c0, c0 + DA_WIDTH)
    for h in range(DA_HEADS):
        vdat_ref[0, h, 0, 0:DA_V_DIM, :] = zv[:, h * DA_V_DIM:(h + 1) * DA_V_DIM].T.astype(BF16)
        vdat_ref[0, h, 0, DA_V_DIM:V_ROWS, :] = sum_rows


def _inproj(h, nb, s, win, gq, wuq, gkv, wukv, tabs_da, tabs_mla):
    m, d = h.shape
    tm = PROJ_ROWS
    ck = min(KV_CHUNK, s)
    tqd = min(DA_Q_TILE, s)
    tqm = min(MLA_Q_TILE, s)
    assert m == nb * s and all(t % tm == 0 and s % t == 0 for t in (tm, ck, tqd, tqm))
    spb = s // tm
    tab_spec = pl.BlockSpec((tm, LANES), lambda i: (i % spb, 0))

    def rows_spec(width):
        return pl.BlockSpec((1, DA_HEADS, tm, width), lambda i: (i // spb, 0, i % spb, 0))

    def cols_spec(rows, tile, *mid):
        per = tile // tm
        zeros = (0,) * (len(mid) + 1)
        return pl.BlockSpec((1, DA_HEADS, 1) + mid + (rows, tm),
                            lambda i: (i // spb, 0, (i % spb) // per) + zeros + ((i % spb) % per,))

    def shape(*tail):
        return jax.ShapeDtypeStruct((nb, DA_HEADS) + tail, BF16)

    vmem = (2 * tm * d * 4 + (win.size + wuq.size + wukv.size) * 2 + 2 * 6 * tm * LANES * 4
            + 2 * tm * 8 * (256 + 128 + 128 + 256 + 256 + 128) * 2 + 8 * tm * 2048 * 4) / MIB + 6
    return pl.pallas_call(
        _inproj_kernel,
        grid=(m // tm,),
        in_specs=[
            pl.BlockSpec((tm, d), lambda i: (i, 0)),
            _resident(win.shape), _resident(gq.shape), _resident(wuq.shape),
            _resident(gkv.shape), _resident(wukv.shape),
            tab_spec, tab_spec, tab_spec, tab_spec, tab_spec, tab_spec,
        ],
        out_specs=[
            cols_spec(LANES, tqd, 2),
            rows_spec(LANES), cols_spec(V_ROWS, ck),
            cols_spec(MLA_QK_PAD, tqm), rows_spec(MLA_QK_PAD), cols_spec(V_ROWS, ck),
        ],
        out_shape=[
            shape(s // tqd, 2, LANES, tqd), shape(s, LANES), shape(s // ck, V_ROWS, ck),
            shape(s // tqm, MLA_QK_PAD, tqm), shape(s, MLA_QK_PAD), shape(s // ck, V_ROWS, ck),
        ],
        compiler_params=_cparams(1, vmem),
        name="inproj",
    )(h, win, gq, wuq, gkv, wukv, *tabs_da, *tabs_mla)


def _masked_meta_scores(kmeta, qt):
    s = jnp.dot(kmeta, qt, preferred_element_type=F32)
    return jnp.where(lax.broadcasted_iota(jnp.int32, s.shape, 0) < N_META, s, NEG_BIG)


def _max_key_norm(k_ref, kmeta, kmax_scr):
    s_len, dk = k_ref.shape[2], k_ref.shape[3]
    rows = min(KV_CHUNK, s_len)

    def max_sq_norm(x):
        xf = x.astype(F32)
        sq = xf * xf
        part = sq[:, 0:LANES]
        for t in range(1, dk // LANES):
            part = part + sq[:, t * LANES:(t + 1) * LANES]
        return jnp.max(jnp.sum(part, axis=1, keepdims=True), axis=0, keepdims=True)

    def body(c, best):
        off = pl.multiple_of(c * rows, rows)
        return jnp.maximum(best, max_sq_norm(k_ref[0, 0, pl.ds(off, rows), :]))

    best = lax.fori_loop(0, s_len // rows, body, max_sq_norm(kmeta))
    kmax_scr[...] = jnp.broadcast_to(jnp.sqrt(best), kmax_scr.shape)


def _score_shift(qt, kmax_scr):
    qf = qt.astype(F32)
    return jnp.sqrt(jnp.sum(qf * qf, axis=0, keepdims=True)) * (kmax_scr[0:1, 0:1] * BOUND_SLACK)


def _flash_cols_shifted(qt, s_meta, shift, k_ref, vt_ref, vmeta_t, acc_scr):
    n_chunks, tk = vt_ref.shape[2], vt_ref.shape[4]
    group = CHUNKS_PER_ITER

    def weights(s):
        return jnp.exp2(s - shift).astype(BF16)

    def chunk_pv(c):
        off = c * tk if isinstance(c, int) else pl.multiple_of(c * tk, tk)
        s = jnp.dot(k_ref[0, 0, pl.ds(off, tk), :], qt, preferred_element_type=F32)
        return jnp.dot(vt_ref[0, 0, c], weights(s), preferred_element_type=F32)

    acc_scr[...] = jnp.zeros_like(acc_scr)

    def body(j, carry):
        total = chunk_pv(group * j)
        for u in range(1, group):
            total = total + chunk_pv(group * j + u)
        acc_scr[...] += total
        return carry

    n_iter = n_chunks // group if n_chunks >= 2 * group else 0
    lax.fori_loop(0, n_iter, body, 0)
    for c in range(group * n_iter, n_chunks):
        acc_scr[...] += chunk_pv(c)
    return acc_scr[...] + jnp.dot(vmeta_t, weights(s_meta), preferred_element_type=F32)


def _flash_cols_online(qt, s_meta, k_ref, vt_ref, vmeta_t, s_scrs, cmax_scrs, m_scr, acc_scr):
    n_chunks, tk = vt_ref.shape[2], vt_ref.shape[4]
    r = qt.shape[1]
    group = ONLINE_CHUNKS_PER_ITER
    assert group % 2 == 0

    def bcast(x):
        return jnp.broadcast_to(x, (SUBLANES, r))

    def scores(c):
        off = c * tk if isinstance(c, int) else pl.multiple_of(c * tk, tk)
        return jnp.dot(k_ref[0, 0, pl.ds(off, tk), :], qt, preferred_element_type=F32)

    def put(s, slot, rows):
        s_scrs[slot][0:rows, :] = s
        cmax_scrs[slot][...] = bcast(jnp.max(s, axis=0, keepdims=True))

    def fold(slot, rows, vt, first=False):
        cmax = cmax_scrs[slot][0:1, :]
        m_new = cmax if first else jnp.maximum(m_scr[0:1, :], cmax)
        p = jnp.exp2(s_scrs[slot][0:rows, :] - m_new)
        pv = jnp.dot(vt, p.astype(BF16), preferred_element_type=F32)
        if first:
            acc_scr[...] = pv
        else:
            acc_scr[...] = jnp.exp2(m_scr[0:1, :] - m_new) * acc_scr[...] + pv
        m_scr[...] = bcast(m_new)

    put(s_meta, 1, META_PAD)
    put(scores(0), 0, tk)
    fold(1, META_PAD, vmeta_t, first=True)

    def step(c, parity, last):
        if not last:
            put(scores(c + 1), 1 - parity, tk)
        fold(parity, tk, vt_ref[0, 0, c])

    n_iter = (n_chunks - 1) // group

    def body(j, carry):
        for u in range(group):
            step(group * j + u, u % 2, False)
        return carry

    lax.fori_loop(0, n_iter, body, 0)
    for c in range(group * n_iter, n_chunks):
        step(c, c % 2, c == n_chunks - 1)
    return acc_scr[...]


def _normalised(acc, dv):
    return acc[0:dv, :] * (1.0 / acc[dv:dv + 1, :])


def _flash_cols(qt, k_ref, vt_ref, kmeta, vmeta_t, finish,
                s0_scr, s1_scr, c0_scr, c1_scr, m_scr, kmax_scr, acc_scr):
    @pl.when(pl.program_id(2) == 0)
    def _():
        _max_key_norm(k_ref, kmeta, kmax_scr)

    s_meta = _masked_meta_scores(kmeta, qt)
    shift = _score_shift(qt, kmax_scr)
    bounded = jnp.max(shift) <= SCORE_BOUND

    @pl.when(bounded)
    def _():
        finish(_flash_cols_shifted(qt, s_meta, shift, k_ref, vt_ref, vmeta_t, acc_scr))

    @pl.when(jnp.logical_not(bounded))
    def _():
        finish(_flash_cols_online(qt, s_meta, k_ref, vt_ref, vmeta_t,
                                  (s0_scr, s1_scr), (c0_scr, c1_scr), m_scr, acc_scr))


def _da_attn_kernel(qt_ref, k_ref, vt_ref, kmeta_ref, vmeta_ref, lq1_ref, lk1_ref, lq2_ref, lk2_ref,
                    g_ref, o_ref, *scratch):
    tq = o_ref.shape[1]
    qt = jnp.concatenate([qt_ref[0, 0, 0, 0], qt_ref[0, 0, 0, 1]], axis=1)

    def finish(acc):
        at = _normalised(acc, DA_V_DIM)
        lam = (jnp.exp(jnp.sum(lq1_ref[...] * lk1_ref[...], axis=1, keepdims=True))
               - jnp.exp(jnp.sum(lq2_ref[...] * lk2_ref[...], axis=1, keepdims=True)) + LAMBDA_INIT)
        ot = at[:, :tq] - lam * at[:, tq:]
        inv = lax.rsqrt(jnp.mean(ot * ot, axis=0, keepdims=True) + RMS_EPS)
        o_ref[0] = ((ot * inv).T * g_ref[...] * (1.0 - LAMBDA_INIT)).astype(BF16)

    _flash_cols(qt, k_ref, vt_ref, kmeta_ref[0, 0], vmeta_ref[0, 0], finish, *scratch)


def _mla_attn_kernel(qt_ref, k_ref, vt_ref, kmeta_ref, vmeta_ref, o_ref, *scratch):
    def finish(acc):
        o_ref[0] = _normalised(acc, MLA_V).T.astype(BF16)

    _flash_cols(qt_ref[0, 0, 0], k_ref, vt_ref, kmeta_ref[0, 0], vmeta_ref[0, 0], finish, *scratch)


def _attn_scratch(cols, tk):
    stat = pltpu.VMEM((SUBLANES, cols), F32)
    return [pltpu.VMEM((tk, cols), F32), pltpu.VMEM((tk, cols), F32), stat, stat, stat,
            pltpu.VMEM((SUBLANES, LANES), F32), pltpu.VMEM((V_ROWS, cols), F32)]


def _attn_vmem(cols, tk, s, dk, dv):
    return (2 * s * (dk + dv) * 2 + 2 * dk * cols * 2 + 2 * tk * cols * 4 + dv * cols * 4
            + 4 * tk * cols * 4 + 2 * cols * dv * 2) / MIB + 8


def _da_attn(qdat, kda, vdat, kmeta, vmeta_t, lam_vecs, g):
    nb, nh, nq, _, _, tq = qdat.shape
    cols = 2 * tq
    s = kda.shape[2]
    nc, tk = vdat.shape[2], vdat.shape[4]
    vec_spec = pl.BlockSpec((1, DA_HEAD_DIM), lambda b, h, i: (0, 0))
    meta_spec = pl.BlockSpec((1, 1, META_PAD, LANES), lambda b, h, i: (0, h, 0, 0))
    vmeta_spec = pl.BlockSpec((1, 1, V_ROWS, META_PAD), lambda b, h, i: (0, h, 0, 0))
    return pl.pallas_call(
        _da_attn_kernel,
        grid=(nb, nh, nq),
        in_specs=[
            pl.BlockSpec((1, 1, 1, 2, LANES, tq), lambda b, h, i: (b, h, i, 0, 0, 0)),
            pl.BlockSpec((1, 1, s, LANES), lambda b, h, i: (b, h, 0, 0)),
            pl.BlockSpec((1, 1, nc, V_ROWS, tk), lambda b, h, i: (b, h, 0, 0, 0)),
            meta_spec, vmeta_spec,
            vec_spec, vec_spec, vec_spec, vec_spec,
            pl.BlockSpec((1, DA_V_DIM), lambda b, h, i: (0, 0)),
        ],
        out_specs=pl.BlockSpec((1, tq, DA_V_DIM), lambda b, h, i: (b, i, h)),
        out_shape=jax.ShapeDtypeStruct((nb, s, DA_WIDTH), BF16),
        scratch_shapes=_attn_scratch(cols, tk),
        compiler_params=_cparams(3, _attn_vmem(cols, tk, s, LANES, DA_V_DIM)),
        name="da_attn",
    )(qdat, kda, vdat, kmeta, vmeta_t, *lam_vecs, g)


def _mla_attn(qmt, km, vmt, kmeta, vmeta_t):
    nb, nh, nq, _, tq = qmt.shape
    s = km.shape[2]
    nc, tk = vmt.shape[2], vmt.shape[4]
    return pl.pallas_call(
        _mla_attn_kernel,
        grid=(nb, nh, nq),
        in_specs=[
            pl.BlockSpec((1, 1, 1, MLA_QK_PAD, tq), lambda b, h, i: (b, h, i, 0, 0)),
            pl.BlockSpec((1, 1, s, MLA_QK_PAD), lambda b, h, i: (b, h, 0, 0)),
            pl.BlockSpec((1, 1, nc, V_ROWS, tk), lambda b, h, i: (b, h, 0, 0, 0)),
            pl.BlockSpec((1, 1, META_PAD, MLA_QK_PAD), lambda b, h, i: (0, h, 0, 0)),
            pl.BlockSpec((1, 1, V_ROWS, META_PAD), lambda b, h, i: (0, h, 0, 0)),
        ],
        out_specs=pl.BlockSpec((1, tq, MLA_V), lambda b, h, i: (b, i, h)),
        out_shape=jax.ShapeDtypeStruct((nb, s, MLA_WIDTH), BF16),
        scratch_shapes=_attn_scratch(tq, tk),
        compiler_params=_cparams(3, _attn_vmem(tq, tk, s, MLA_QK_PAD, MLA_V)),
        name="mla_attn",
    )(qmt, km, vmt, kmeta, vmeta_t)


def _outproj_ln_kernel(h_ref, oa_ref, om_ref, wo_ref, g_ref, b_ref, o_ref):
    tm = h_ref.shape[0]
    sub = tm // OUTPROJ_SUBBLOCKS if tm % (OUTPROJ_SUBBLOCKS * 16) == 0 else tm
    for r0 in range(0, tm, sub):
        rows = pl.ds(r0, sub)
        mix = (jnp.dot(oa_ref[rows, :], wo_ref[0:DA_WIDTH, :], preferred_element_type=F32)
               + jnp.dot(om_ref[rows, :], wo_ref[DA_WIDTH:DA_WIDTH + MLA_WIDTH, :], preferred_element_type=F32))
        o_ref[rows, :] = _layer_norm(ALPHA * h_ref[rows, :] + mix, g_ref[...], b_ref[...])


def _outproj_ln(h, oa, om, wo, g, b, *, tm=512):
    m, d = h.shape
    tm = min(tm, m)
    assert m % tm == 0
    vmem = (2 * 2 * tm * d * 4 + 2 * 2 * tm * DA_WIDTH * 2 + wo.size * 2 + 3 * tm * d * 4) / MIB + 6
    return pl.pallas_call(
        _outproj_ln_kernel,
        grid=(m // tm,),
        in_specs=[
            pl.BlockSpec((tm, d), lambda i: (i, 0)),
            pl.BlockSpec((tm, DA_WIDTH), lambda i: (i, 0)),
            pl.BlockSpec((tm, MLA_WIDTH), lambda i: (i, 0)),
            _resident(wo.shape), _resident(g.shape), _resident(b.shape),
        ],
        out_specs=pl.BlockSpec((tm, d), lambda i: (i, 0)),
        out_shape=jax.ShapeDtypeStruct((m, d), F32),
        compiler_params=_cparams(1, vmem),
        name="outproj_ln",
    )(h, oa, om, wo, g, b)


def _rope_lane_tables(n_pos, rot_dim, period):
    half = rot_dim // 2
    inv = 1.0 / (ROPE_THETA ** (jnp.arange(0, rot_dim, 2, dtype=F32) / rot_dim))
    ang = jnp.arange(n_pos, dtype=F32)[:, None] * inv[None, :]
    cos, sin = jnp.cos(ang), jnp.sin(ang)
    within = jnp.arange(LANES) % period
    idx = within % half
    c = jnp.where(within < rot_dim, cos[:, idx], 1.0)
    sp = jnp.where((within >= half) & (within < rot_dim), sin[:, idx], 0.0)
    sm = jnp.where(within < half, -sin[:, idx], 0.0)
    return c, sp, sm


def _prep_weights(p):
    w = {}
    for name in ("ffn1_w_gate", "ffn1_w_up", "ffn2_w_gate", "ffn2_w_up", "w_o", "w_ukv"):
        w[name] = p[name][0].astype(BF16)
    for name in ("ffn1_w_down", "ffn2_w_down"):
        w[name] = (p[name][0] * 0.5).astype(BF16)
    w_in = p["w_in"][0]
    w["w_in"] = jnp.pad(w_in, ((0, 0), (0, LANES - MLA_ROPE))).astype(BF16)
    w_uq = p["w_uq"][0].reshape(MLA_Q_RANK, MLA_HEADS, MLA_NOPE + MLA_ROPE)
    w_uq = jnp.pad(w_uq, ((0, 0), (0, 0), (0, MLA_QK_PAD - MLA_NOPE - MLA_ROPE)))
    w["w_uq"] = w_uq.reshape(MLA_Q_RANK, MLA_HEADS * MLA_QK_PAD).astype(BF16)
    for name in ("ln1_g", "ln1_b", "ln2_g", "ln2_b", "ln3_g", "ln3_b", "da_subln_g", "mla_q_norm_g", "mla_kv_norm_g",
                 "da_lambda_q1", "da_lambda_k1", "da_lambda_q2", "da_lambda_k2"):
        w[name] = p[name].astype(F32)
    return w


def _mixer_inputs(h, nb, s, w, tabs_da, tabs_mla):
    return _inproj(h, nb, s, w["w_in"], w["mla_q_norm_g"], w["w_uq"], w["mla_kv_norm_g"], w["w_ukv"],
                   tabs_da, tabs_mla)


def _trunk(x, meta_kv, w, tabs_da, tabs_mla):
    nb, s, d = x.shape
    h1 = _ffn_ln(x.reshape(nb * s, d), w["ffn1_w_gate"], w["ffn1_w_up"], w["ffn1_w_down"], w["ln1_g"], w["ln1_b"])
    qdat, kda, vdat, qmt, km, vmt = _mixer_inputs(h1, nb, s, w, tabs_da, tabs_mla)
    kda_m, vdat_m, km_m, vmt_m = meta_kv
    lam_vecs = (w["da_lambda_q1"], w["da_lambda_k1"], w["da_lambda_q2"], w["da_lambda_k2"])
    oa = _da_attn(qdat, kda, vdat, kda_m, vdat_m, lam_vecs, w["da_subln_g"])
    om = _mla_attn(qmt, km, vmt, km_m, vmt_m)
    h2 = _outproj_ln(h1, oa.reshape(nb * s, DA_WIDTH), om.reshape(nb * s, MLA_WIDTH), w["w_o"], w["ln2_g"], w["ln2_b"])
    y = _ffn_ln(h2, w["ffn2_w_gate"], w["ffn2_w_up"], w["ffn2_w_down"], w["ln3_g"], w["ln3_b"])
    return y.reshape(nb, s, d)


def kernel(x_prompt, x_sample, meta_tokens, ffn1_w_gate, ffn1_w_up, ffn1_w_down, ln1_g, ln1_b, w_in, da_lambda_q1, da_lambda_k1, da_lambda_q2, da_lambda_k2, da_subln_g, mla_q_norm_g, w_uq, mla_kv_norm_g, w_ukv, w_o, ln2_g, ln2_b, ffn2_w_gate, ffn2_w_up, ffn2_w_down, ln3_g, ln3_b):
    p = dict(ffn1_w_gate=ffn1_w_gate, ffn1_w_up=ffn1_w_up, ffn1_w_down=ffn1_w_down, ln1_g=ln1_g, ln1_b=ln1_b,
             w_in=w_in, da_lambda_q1=da_lambda_q1, da_lambda_k1=da_lambda_k1, da_lambda_q2=da_lambda_q2,
             da_lambda_k2=da_lambda_k2, da_subln_g=da_subln_g, mla_q_norm_g=mla_q_norm_g, w_uq=w_uq,
             mla_kv_norm_g=mla_kv_norm_g, w_ukv=w_ukv, w_o=w_o, ln2_g=ln2_g, ln2_b=ln2_b,
             ffn2_w_gate=ffn2_w_gate, ffn2_w_up=ffn2_w_up, ffn2_w_down=ffn2_w_down, ln3_g=ln3_g, ln3_b=ln3_b)
    w = _prep_weights(p)
    s_max = max(x_prompt.shape[1], x_sample.shape[1])
    tabs_da = _rope_lane_tables(N_META + s_max, DA_ROT, DA_HEAD_DIM)
    tabs_mla = _rope_lane_tables(N_META + s_max, MLA_ROPE, LANES)

    hm = _ffn_ln(meta_tokens.astype(F32), w["ffn1_w_gate"], w["ffn1_w_up"], w["ffn1_w_down"], w["ln1_g"], w["ln1_b"])
    hm = jnp.pad(hm, ((0, PROJ_ROWS - N_META), (0, 0)))
    _, kda_m, vdat_m, _, km_m, vmt_m = _mixer_inputs(
        hm, 1, PROJ_ROWS, w, [t[:PROJ_ROWS] for t in tabs_da], [t[:PROJ_ROWS] for t in tabs_mla])
    meta_kv = (kda_m[:, :, :META_PAD], vdat_m[:, :, 0, :, :META_PAD], km_m[:, :, :META_PAD], vmt_m[:, :, 0, :, :META_PAD])

    seq_da = [t[N_META:] for t in tabs_da]
    seq_mla = [t[N_META:] for t in tabs_mla]
    y_prompt = _trunk(x_prompt, meta_kv, w, seq_da, seq_mla)
    y_sample = _trunk(x_sample, meta_kv, w, seq_da, seq_mla)
    return (y_prompt, y_sample)
```

```python
import functools
import math

import jax
import jax.numpy as jnp
from jax import lax
from jax.experimental import pallas as pl
from jax.experimental.pallas import tpu as pltpu

N_META = 16
ROPE_THETA = 500000.0
DA_HEADS = 8
DA_HEAD_DIM = 64
DA_V_DIM = 2 * DA_HEAD_DIM
DA_ROT = DA_HEAD_DIM // 4
DA_QK_COLS = DA_HEADS * 2 * DA_HEAD_DIM
DA_WIDTH = DA_HEADS * DA_V_DIM
MLA_HEADS = 8
MLA_Q_RANK = 512
MLA_KV_RANK = 256
MLA_NOPE = 128
MLA_ROPE = 64
MLA_V = 128
MLA_WIDTH = MLA_HEADS * MLA_V
MLA_QK_PAD = 256
DEPTH = 1
ALPHA = (2 * DEPTH) ** 0.25
LN_EPS = 1e-5
RMS_EPS = 1e-6
LAMBDA_INIT = 0.8 - 0.6 * math.exp(-0.3 * 0)
LOG2E = 1.4426950408889634

LANES = 128
SUBLANES = 8
META_PAD = 128
NEG_BIG = -1e30

PROJ_ROWS = 256
DA_Q_TILE = 1024
MLA_Q_TILE = 2048
KV_CHUNK = 512
CHUNKS_PER_ITER = 8
ONLINE_CHUNKS_PER_ITER = 2
OUTPROJ_SUBBLOCKS = 2
SCORE_BOUND = 45.0
BOUND_SLACK = 1.01
V_EXTRA_ROWS = 16
V_ROWS = DA_V_DIM + V_EXTRA_ROWS

F32 = jnp.float32
BF16 = jnp.bfloat16
MIB = 1024 * 1024


def _cparams(n_axes, vmem_mib):
    return pltpu.CompilerParams(
        dimension_semantics=("arbitrary",) * n_axes,
        vmem_limit_bytes=int(vmem_mib * MIB),
    )


def _resident(shape):
    n = len(shape)
    return pl.BlockSpec(shape, lambda *_: (0,) * n, pipeline_mode=pl.Buffered(1))


def _layer_norm(y, g, b):
    mu = jnp.mean(y, axis=-1, keepdims=True)
    d = y - mu
    var = jnp.mean(d * d, axis=-1, keepdims=True)
    return d * lax.rsqrt(var + LN_EPS) * g + b


def _rms_norm(x, g):
    return x * lax.rsqrt(jnp.mean(x * x, axis=-1, keepdims=True) + RMS_EPS) * g


def _ffn_ln_kernel(x_ref, wg_ref, wu_ref, wd_ref, g_ref, b_ref, o_ref, xb_ref, acc_ref):
    j = pl.program_id(1)

    @pl.when(j == 0)
    def _():
        xb_ref[...] = x_ref[...].astype(BF16)
        acc_ref[...] = jnp.zeros_like(acc_ref)

    xb = xb_ref[...]
    gate = jnp.dot(xb, wg_ref[...], preferred_element_type=F32)
    up = jnp.dot(xb, wu_ref[...], preferred_element_type=F32)
    hid = (gate * (1.0 / (1.0 + jnp.exp(-gate)))) * up
    acc_ref[...] += jnp.dot(hid.astype(BF16), wd_ref[...], preferred_element_type=F32)

    @pl.when(j == pl.num_programs(1) - 1)
    def _():
        y = ALPHA * x_ref[...] + acc_ref[...]
        o_ref[...] = _layer_norm(y, g_ref[...], b_ref[...])


def _ffn_ln(x, wg, wu, wd, g, b, *, tm=512, tf=512):
    m, d = x.shape
    f = wg.shape[1]
    tm = min(tm, m)
    tf = min(tf, f)
    assert m % tm == 0 and f % tf == 0
    vmem = (2 * 2 * tm * d * 4 + 2 * 3 * d * tf * 2 + tm * d * 6 + 4 * tm * tf * 4) / MIB + 6
    return pl.pallas_call(
        _ffn_ln_kernel,
        grid=(m // tm, f // tf),
        in_specs=[
            pl.BlockSpec((tm, d), lambda i, j: (i, 0)),
            pl.BlockSpec((d, tf), lambda i, j: (0, j)),
            pl.BlockSpec((d, tf), lambda i, j: (0, j)),
            pl.BlockSpec((tf, d), lambda i, j: (j, 0)),
            pl.BlockSpec((1, d), lambda i, j: (0, 0)),
            pl.BlockSpec((1, d), lambda i, j: (0, 0)),
        ],
        out_specs=pl.BlockSpec((tm, d), lambda i, j: (i, 0)),
        out_shape=jax.ShapeDtypeStruct((m, d), F32),
        scratch_shapes=[pltpu.VMEM((tm, d), BF16), pltpu.VMEM((tm, d), F32)],
        compiler_params=_cparams(2, vmem),
        name="ffn_ln",
    )(x, wg, wu, wd, g, b)


def _rope(x, c, sp, sm, half):
    return x * c + pltpu.roll(x, half, 1) * sp + pltpu.roll(x, LANES - half, 1) * sm


def _inproj_kernel(h_ref, win_ref, gq_ref, wuq_ref, gkv_ref, wukv_ref,
                   cda_ref, spda_ref, smda_ref, cm_ref, spm_ref, smm_ref,
                   qdat_ref, kda_ref, vdat_ref, qmt_ref, km_ref, vmt_ref, knorm_ref):
    tm = h_ref.shape[0]
    hb = h_ref[...].astype(BF16)

    def proj(lo, hi):
        return jnp.dot(hb, win_ref[:, lo:hi], preferred_element_type=F32)

    def row_sq_norm(x):
        return jnp.sum(x * x, axis=1, keepdims=True)

    def put_max_sq_norm(row, n2):
        knorm_ref[0, row:row + 1, :] = jnp.broadcast_to(jnp.max(n2, axis=0, keepdims=True), (1, LANES))

    cda, spda, smda = cda_ref[...], spda_ref[...], smda_ref[...]
    cm, spm, smm = cm_ref[...], spm_ref[...], smm_ref[...]
    da_half = DA_ROT // 2
    mla_half = MLA_ROPE // 2
    map1_dims = lax.broadcasted_iota(jnp.int32, (LANES, tm), 0) < DA_HEAD_DIM
    da_scale = DA_HEAD_DIM ** -0.5 * LOG2E
    mla_scale = (MLA_NOPE + MLA_ROPE) ** -0.5 * LOG2E
    sum_rows = (lax.broadcasted_iota(jnp.int32, (V_EXTRA_ROWS, tm), 0) == 0).astype(BF16)

    c0 = 2 * DA_QK_COLS + DA_WIDTH
    cq = _rms_norm(proj(c0, c0 + MLA_Q_RANK), gq_ref[...]).astype(BF16)
    c0 += MLA_Q_RANK
    qm = jnp.dot(cq, wuq_ref[...], preferred_element_type=F32) * mla_scale
    for h in range(MLA_HEADS):
        base = h * MLA_QK_PAD
        qmt_ref[0, h, 0, 0:MLA_NOPE, :] = qm[:, base:base + MLA_NOPE].T.astype(BF16)
        qmt_ref[0, h, 0, MLA_NOPE:MLA_QK_PAD, :] = _rope(
            qm[:, base + MLA_NOPE:base + MLA_QK_PAD], cm, spm, smm, mla_half).T.astype(BF16)

    ckv = _rms_norm(proj(c0, c0 + MLA_KV_RANK), gkv_ref[...]).astype(BF16)
    c0 += MLA_KV_RANK
    kv = jnp.dot(ckv, wukv_ref[...], preferred_element_type=F32)
    kr_f = _rope(proj(c0, c0 + LANES), cm, spm, smm, mla_half)
    kr = kr_f.astype(BF16)
    kr_n2 = row_sq_norm(kr_f)
    for h in range(MLA_HEADS):
        base = h * (MLA_NOPE + MLA_V)
        k_nope = kv[:, base:base + MLA_NOPE]
        km_ref[0, h, :, 0:MLA_NOPE] = k_nope.astype(BF16)
        km_ref[0, h, :, MLA_NOPE:MLA_QK_PAD] = kr
        put_max_sq_norm(DA_HEADS + h, row_sq_norm(k_nope) + kr_n2)
        vmt_ref[0, h, 0, 0:MLA_V, :] = kv[:, base + MLA_NOPE:base + MLA_NOPE + MLA_V].T.astype(BF16)
        vmt_ref[0, h, 0, MLA_V:V_ROWS, :] = sum_rows

    c0 = 0
    zq = proj(c0, c0 + DA_QK_COLS)
    for h in range(DA_HEADS):
        rt = (_rope(zq[:, h * LANES:(h + 1) * LANES], cda, spda, smda, da_half) * da_scale).T
        qdat_ref[0, h, 0, 0] = jnp.where(map1_dims, rt, 0.0).astype(BF16)
        qdat_ref[0, h, 0, 1] = jnp.where(map1_dims, 0.0, rt).astype(BF16)
    c0 += DA_QK_COLS
    zk = proj(c0, c0 + DA_QK_COLS)
    for h in range(DA_HEADS):
        k = _rope(zk[:, h * LANES:(h + 1) * LANES], cda, spda, smda, da_half)
        kda_ref[0, h] = k.astype(BF16)
        put_max_sq_norm(h, row_sq_norm(k))
    c0 += DA_QK_COLS
    zv = proj(c0, c0 + DA_WIDTH)
    for h in range(DA_HEADS):
        vdat_ref[0, h, 0, 0:DA_V_DIM, :] = zv[:, h * DA_V_DIM:(h + 1) * DA_V_DIM].T.astype(BF16)
        vdat_ref[0, h, 0, DA_V_DIM:V_ROWS, :] = sum_rows


def _inproj(h, nb, s, win, gq, wuq, gkv, wukv, tabs_da, tabs_mla):
    m, d = h.shape
    tm = PROJ_ROWS
    ck = min(KV_CHUNK, s)
    tqd = min(DA_Q_TILE, s)
    tqm = min(MLA_Q_TILE, s)
    assert m == nb * s and all(t % tm == 0 and s % t == 0 for t in (tm, ck, tqd, tqm))
    spb = s // tm
    tab_spec = pl.BlockSpec((tm, LANES), lambda i: (i % spb, 0))

    def rows_spec(width):
        return pl.BlockSpec((1, DA_HEADS, tm, width), lambda i: (i // spb, 0, i % spb, 0))

    def cols_spec(rows, tile, *mid):
        per = tile // tm
        zeros = (0,) * (len(mid) + 1)
        return pl.BlockSpec((1, DA_HEADS, 1) + mid + (rows, tm),
                            lambda i: (i // spb, 0, (i % spb) // per) + zeros + ((i % spb) % per,))

    def shape(*tail):
        return jax.ShapeDtypeStruct((nb, DA_HEADS) + tail, BF16)

    vmem = (2 * tm * d * 4 + (win.size + wuq.size + wukv.size) * 2 + 2 * 6 * tm * LANES * 4
            + 2 * tm * 8 * (256 + 128 + 128 + 256 + 256 + 128) * 2 + 8 * tm * 2048 * 4) / MIB + 6
    return pl.pallas_call(
        _inproj_kernel,
        grid=(m // tm,),
        in_specs=[
            pl.BlockSpec((tm, d), lambda i: (i, 0)),
            _resident(win.shape), _resident(gq.shape), _resident(wuq.shape),
            _resident(gkv.shape), _resident(wukv.shape),
            tab_spec, tab_spec, tab_spec, tab_spec, tab_spec, tab_spec,
        ],
        out_specs=[
            cols_spec(LANES, tqd, 2),
            rows_spec(LANES), cols_spec(V_ROWS, ck),
            cols_spec(MLA_QK_PAD, tqm), rows_spec(MLA_QK_PAD), cols_spec(V_ROWS, ck),
            pl.BlockSpec((1, DA_HEADS + MLA_HEADS, LANES), lambda i: (i, 0, 0)),
        ],
        out_shape=[
            shape(s // tqd, 2, LANES, tqd), shape(s, LANES), shape(s // ck, V_ROWS, ck),
            shape(s // tqm, MLA_QK_PAD, tqm), shape(s, MLA_QK_PAD), shape(s // ck, V_ROWS, ck),
            jax.ShapeDtypeStruct((m // tm, DA_HEADS + MLA_HEADS, LANES), F32),
        ],
        compiler_params=_cparams(1, vmem),
        name="inproj",
    )(h, win, gq, wuq, gkv, wukv, *tabs_da, *tabs_mla)


def _masked_meta_scores(kmeta, qt):
    s = jnp.dot(kmeta, qt, preferred_element_type=F32)
    return jnp.where(lax.broadcasted_iota(jnp.int32, s.shape, 0) < N_META, s, NEG_BIG)


def _score_shift(qt, kmax):
    qf = qt.astype(F32)
    return jnp.sqrt(jnp.sum(qf * qf, axis=0, keepdims=True)) * (kmax[0:1, 0:1] * BOUND_SLACK)


def _flash_cols_shifted(qt, s_meta, shift, k_ref, vt_ref, vmeta_t, acc_scr):
    n_chunks, tk = vt_ref.shape[2], vt_ref.shape[4]
    group = CHUNKS_PER_ITER

    def weights(s):
        return jnp.exp2(s - shift).astype(BF16)

    def chunk_pv(c):
        off = c * tk if isinstance(c, int) else pl.multiple_of(c * tk, tk)
        s = jnp.dot(k_ref[0, 0, pl.ds(off, tk), :], qt, preferred_element_type=F32)
        return jnp.dot(vt_ref[0, 0, c], weights(s), preferred_element_type=F32)

    acc_scr[...] = jnp.zeros_like(acc_scr)

    def body(j, carry):
        total = chunk_pv(group * j)
        for u in range(1, group):
            total = total + chunk_pv(group * j + u)
        acc_scr[...] += total
        return carry

    n_iter = n_chunks // group if n_chunks >= 2 * group else 0
    lax.fori_loop(0, n_iter, body, 0)
    for c in range(group * n_iter, n_chunks):
        acc_scr[...] += chunk_pv(c)
    return acc_scr[...] + jnp.dot(vmeta_t, weights(s_meta), preferred_element_type=F32)


def _flash_cols_online(qt, s_meta, k_ref, vt_ref, vmeta_t, s_scrs, cmax_scrs, m_scr, acc_scr):
    n_chunks, tk = vt_ref.shape[2], vt_ref.shape[4]
    r = qt.shape[1]
    group = ONLINE_CHUNKS_PER_ITER
    assert group % 2 == 0

    def bcast(x):
        return jnp.broadcast_to(x, (SUBLANES, r))

    def scores(c):
        off = c * tk if isinstance(c, int) else pl.multiple_of(c * tk, tk)
        return jnp.dot(k_ref[0, 0, pl.ds(off, tk), :], qt, preferred_element_type=F32)

    def put(s, slot, rows):
        s_scrs[slot][0:rows, :] = s
        cmax_scrs[slot][...] = bcast(jnp.max(s, axis=0, keepdims=True))

    def fold(slot, rows, vt, first=False):
        cmax = cmax_scrs[slot][0:1, :]
        m_new = cmax if first else jnp.maximum(m_scr[0:1, :], cmax)
        p = jnp.exp2(s_scrs[slot][0:rows, :] - m_new)
        pv = jnp.dot(vt, p.astype(BF16), preferred_element_type=F32)
        if first:
            acc_scr[...] = pv
        else:
            acc_scr[...] = jnp.exp2(m_scr[0:1, :] - m_new) * acc_scr[...] + pv
        m_scr[...] = bcast(m_new)

    put(s_meta, 1, META_PAD)
    put(scores(0), 0, tk)
    fold(1, META_PAD, vmeta_t, first=True)

    def step(c, parity, last):
        if not last:
            put(scores(c + 1), 1 - parity, tk)
        fold(parity, tk, vt_ref[0, 0, c])

    n_iter = (n_chunks - 1) // group

    def body(j, carry):
        for u in range(group):
            step(group * j + u, u % 2, False)
        return carry

    lax.fori_loop(0, n_iter, body, 0)
    for c in range(group * n_iter, n_chunks):
        step(c, c % 2, c == n_chunks - 1)
    return acc_scr[...]


def _normalised(acc, dv):
    return acc[0:dv, :] * (1.0 / acc[dv:dv + 1, :])


def _flash_cols(qt, k_ref, vt_ref, kmeta, vmeta_t, kmax, finish,
                s0_scr, s1_scr, c0_scr, c1_scr, m_scr, acc_scr):
    s_meta = _masked_meta_scores(kmeta, qt)
    shift = _score_shift(qt, kmax)
    bounded = jnp.max(shift) <= SCORE_BOUND

    @pl.when(bounded)
    def _():
        finish(_flash_cols_shifted(qt, s_meta, shift, k_ref, vt_ref, vmeta_t, acc_scr))

    @pl.when(jnp.logical_not(bounded))
    def _():
        finish(_flash_cols_online(qt, s_meta, k_ref, vt_ref, vmeta_t,
                                  (s0_scr, s1_scr), (c0_scr, c1_scr), m_scr, acc_scr))


def _da_attn_kernel(qt_ref, k_ref, vt_ref, kmeta_ref, vmeta_ref, kmax_ref, lq1_ref, lk1_ref, lq2_ref, lk2_ref,
                    g_ref, o_ref, *scratch):
    tq = o_ref.shape[1]
    qt = jnp.concatenate([qt_ref[0, 0, 0, 0], qt_ref[0, 0, 0, 1]], axis=1)

    def finish(acc):
        at = _normalised(acc, DA_V_DIM)
        lam = (jnp.exp(jnp.sum(lq1_ref[...] * lk1_ref[...], axis=1, keepdims=True))
               - jnp.exp(jnp.sum(lq2_ref[...] * lk2_ref[...], axis=1, keepdims=True)) + LAMBDA_INIT)
        ot = at[:, :tq] - lam * at[:, tq:]
        inv = lax.rsqrt(jnp.mean(ot * ot, axis=0, keepdims=True) + RMS_EPS)
        o_ref[0] = ((ot * inv).T * g_ref[...] * (1.0 - LAMBDA_INIT)).astype(BF16)

    _flash_cols(qt, k_ref, vt_ref, kmeta_ref[0, 0], vmeta_ref[0, 0], kmax_ref[0, 0], finish, *scratch)


def _mla_attn_kernel(qt_ref, k_ref, vt_ref, kmeta_ref, vmeta_ref, kmax_ref, o_ref, *scratch):
    def finish(acc):
        o_ref[0] = _normalised(acc, MLA_V).T.astype(BF16)

    _flash_cols(qt_ref[0, 0, 0], k_ref, vt_ref, kmeta_ref[0, 0], vmeta_ref[0, 0], kmax_ref[0, 0], finish, *scratch)


def _attn_scratch(cols, tk):
    stat = pltpu.VMEM((SUBLANES, cols), F32)
    return [pltpu.VMEM((tk, cols), F32), pltpu.VMEM((tk, cols), F32), stat, stat, stat,
            pltpu.VMEM((V_ROWS, cols), F32)]


def _kmax_spec():
    return pl.BlockSpec((1, 1, 1, LANES), lambda b, h, i: (b, h, 0, 0))


def _attn_vmem(cols, tk, s, dk, dv):
    return (2 * s * (dk + dv) * 2 + 2 * dk * cols * 2 + 2 * tk * cols * 4 + dv * cols * 4
            + 4 * tk * cols * 4 + 2 * cols * dv * 2) / MIB + 8


def _da_attn(qdat, kda, vdat, kmeta, vmeta_t, kmax, lam_vecs, g):
    nb, nh, nq, _, _, tq = qdat.shape
    cols = 2 * tq
    s = kda.shape[2]
    nc, tk = vdat.shape[2], vdat.shape[4]
    vec_spec = pl.BlockSpec((1, DA_HEAD_DIM), lambda b, h, i: (0, 0))
    meta_spec = pl.BlockSpec((1, 1, META_PAD, LANES), lambda b, h, i: (0, h, 0, 0))
    vmeta_spec = pl.BlockSpec((1, 1, V_ROWS, META_PAD), lambda b, h, i: (0, h, 0, 0))
    return pl.pallas_call(
        _da_attn_kernel,
        grid=(nb, nh, nq),
        in_specs=[
            pl.BlockSpec((1, 1, 1, 2, LANES, tq), lambda b, h, i: (b, h, i, 0, 0, 0)),
            pl.BlockSpec((1, 1, s, LANES), lambda b, h, i: (b, h, 0, 0)),
            pl.BlockSpec((1, 1, nc, V_ROWS, tk), lambda b, h, i: (b, h, 0, 0, 0)),
            meta_spec, vmeta_spec, _kmax_spec(),
            vec_spec, vec_spec, vec_spec, vec_spec,
            pl.BlockSpec((1, DA_V_DIM), lambda b, h, i: (0, 0)),
        ],
        out_specs=pl.BlockSpec((1, tq, DA_V_DIM), lambda b, h, i: (b, i, h)),
        out_shape=jax.ShapeDtypeStruct((nb, s, DA_WIDTH), BF16),
        scratch_shapes=_attn_scratch(cols, tk),
        compiler_params=_cparams(3, _attn_vmem(cols, tk, s, LANES, DA_V_DIM)),
        name="da_attn",
    )(qdat, kda, vdat, kmeta, vmeta_t, kmax, *lam_vecs, g)


def _mla_attn(qmt, km, vmt, kmeta, vmeta_t, kmax):
    nb, nh, nq, _, tq = qmt.shape
    s = km.shape[2]
    nc, tk = vmt.shape[2], vmt.shape[4]
    return pl.pallas_call(
        _mla_attn_kernel,
        grid=(nb, nh, nq),
        in_specs=[
            pl.BlockSpec((1, 1, 1, MLA_QK_PAD, tq), lambda b, h, i: (b, h, i, 0, 0)),
            pl.BlockSpec((1, 1, s, MLA_QK_PAD), lambda b, h, i: (b, h, 0, 0)),
            pl.BlockSpec((1, 1, nc, V_ROWS, tk), lambda b, h, i: (b, h, 0, 0, 0)),
            pl.BlockSpec((1, 1, META_PAD, MLA_QK_PAD), lambda b, h, i: (0, h, 0, 0)),
            pl.BlockSpec((1, 1, V_ROWS, META_PAD), lambda b, h, i: (0, h, 0, 0)),
            _kmax_spec(),
        ],
        out_specs=pl.BlockSpec((1, tq, MLA_V), lambda b, h, i: (b, i, h)),
        out_shape=jax.ShapeDtypeStruct((nb, s, MLA_WIDTH), BF16),
        scratch_shapes=_attn_scratch(tq, tk),
        compiler_params=_cparams(3, _attn_vmem(tq, tk, s, MLA_QK_PAD, MLA_V)),
        name="mla_attn",
    )(qmt, km, vmt, kmeta, vmeta_t, kmax)


def _outproj_ln_kernel(h_ref, oa_ref, om_ref, wo_ref, g_ref, b_ref, o_ref):
    tm = h_ref.shape[0]
    sub = tm // OUTPROJ_SUBBLOCKS if tm % (OUTPROJ_SUBBLOCKS * 16) == 0 else tm
    for r0 in range(0, tm, sub):
        rows = pl.ds(r0, sub)
        mix = (jnp.dot(oa_ref[rows, :], wo_ref[0:DA_WIDTH, :], preferred_element_type=F32)
               + jnp.dot(om_ref[rows, :], wo_ref[DA_WIDTH:DA_WIDTH + MLA_WIDTH, :], preferred_element_type=F32))
        o_ref[rows, :] = _layer_norm(ALPHA * h_ref[rows, :] + mix, g_ref[...], b_ref[...])


def _outproj_ln(h, oa, om, wo, g, b, *, tm=512):
    m, d = h.shape
    tm = min(tm, m)
    assert m % tm == 0
    vmem = (2 * 2 * tm * d * 4 + 2 * 2 * tm * DA_WIDTH * 2 + wo.size * 2 + 3 * tm * d * 4) / MIB + 6
    return pl.pallas_call(
        _outproj_ln_kernel,
        grid=(m // tm,),
        in_specs=[
            pl.BlockSpec((tm, d), lambda i: (i, 0)),
            pl.BlockSpec((tm, DA_WIDTH), lambda i: (i, 0)),
            pl.BlockSpec((tm, MLA_WIDTH), lambda i: (i, 0)),
            _resident(wo.shape), _resident(g.shape), _resident(b.shape),
        ],
        out_specs=pl.BlockSpec((tm, d), lambda i: (i, 0)),
        out_shape=jax.ShapeDtypeStruct((m, d), F32),
        compiler_params=_cparams(1, vmem),
        name="outproj_ln",
    )(h, oa, om, wo, g, b)


def _rope_lane_tables(n_pos, rot_dim, period):
    half = rot_dim // 2
    inv = 1.0 / (ROPE_THETA ** (jnp.arange(0, rot_dim, 2, dtype=F32) / rot_dim))
    ang = jnp.arange(n_pos, dtype=F32)[:, None] * inv[None, :]
    cos, sin = jnp.cos(ang), jnp.sin(ang)
    within = jnp.arange(LANES) % period
    idx = within % half
    c = jnp.where(within < rot_dim, cos[:, idx], 1.0)
    sp = jnp.where((within >= half) & (within < rot_dim), sin[:, idx], 0.0)
    sm = jnp.where(within < half, -sin[:, idx], 0.0)
    return c, sp, sm


def _prep_weights(p):
    w = {}
    for name in ("ffn1_w_gate", "ffn1_w_up", "ffn2_w_gate", "ffn2_w_up", "w_o", "w_ukv"):
        w[name] = p[name][0].astype(BF16)
    for name in ("ffn1_w_down", "ffn2_w_down"):
        w[name] = (p[name][0] * 0.5).astype(BF16)
    w_in = p["w_in"][0]
    w["w_in"] = jnp.pad(w_in, ((0, 0), (0, LANES - MLA_ROPE))).astype(BF16)
    w_uq = p["w_uq"][0].reshape(MLA_Q_RANK, MLA_HEADS, MLA_NOPE + MLA_ROPE)
    w_uq = jnp.pad(w_uq, ((0, 0), (0, 0), (0, MLA_QK_PAD - MLA_NOPE - MLA_ROPE)))
    w["w_uq"] = w_uq.reshape(MLA_Q_RANK, MLA_HEADS * MLA_QK_PAD).astype(BF16)
    for name in ("ln1_g", "ln1_b", "ln2_g", "ln2_b", "ln3_g", "ln3_b", "da_subln_g", "mla_q_norm_g", "mla_kv_norm_g",
                 "da_lambda_q1", "da_lambda_k1", "da_lambda_q2", "da_lambda_k2"):
        w[name] = p[name].astype(F32)
    return w


def _mixer_inputs(h, nb, s, w, tabs_da, tabs_mla):
    return _inproj(h, nb, s, w["w_in"], w["mla_q_norm_g"], w["w_uq"], w["mla_kv_norm_g"], w["w_ukv"],
                   tabs_da, tabs_mla)


def _trunk(x, meta_kv, w, tabs_da, tabs_mla):
    nb, s, d = x.shape
    h1 = _ffn_ln(x.reshape(nb * s, d), w["ffn1_w_gate"], w["ffn1_w_up"], w["ffn1_w_down"], w["ln1_g"], w["ln1_b"])
    qdat, kda, vdat, qmt, km, vmt, knorm2 = _mixer_inputs(h1, nb, s, w, tabs_da, tabs_mla)
    kda_m, vdat_m, km_m, vmt_m, knorm2_m = meta_kv
    knorm2 = jnp.max(knorm2.reshape(nb, -1, DA_HEADS + MLA_HEADS, LANES), axis=1)
    kmax = jnp.sqrt(jnp.maximum(knorm2, knorm2_m))[:, :, None, :]
    lam_vecs = (w["da_lambda_q1"], w["da_lambda_k1"], w["da_lambda_q2"], w["da_lambda_k2"])
    oa = _da_attn(qdat, kda, vdat, kda_m, vdat_m, kmax[:, :DA_HEADS], lam_vecs, w["da_subln_g"])
    om = _mla_attn(qmt, km, vmt, km_m, vmt_m, kmax[:, DA_HEADS:])
    h2 = _outproj_ln(h1, oa.reshape(nb * s, DA_WIDTH), om.reshape(nb * s, MLA_WIDTH), w["w_o"], w["ln2_g"], w["ln2_b"])
    y = _ffn_ln(h2, w["ffn2_w_gate"], w["ffn2_w_up"], w["ffn2_w_down"], w["ln3_g"], w["ln3_b"])
    return y.reshape(nb, s, d)


def kernel(x_prompt, x_sample, meta_tokens, ffn1_w_gate, ffn1_w_up, ffn1_w_down, ln1_g, ln1_b, w_in, da_lambda_q1, da_lambda_k1, da_lambda_q2, da_lambda_k2, da_subln_g, mla_q_norm_g, w_uq, mla_kv_norm_g, w_ukv, w_o, ln2_g, ln2_b, ffn2_w_gate, ffn2_w_up, ffn2_w_down, ln3_g, ln3_b):
    p = dict(ffn1_w_gate=ffn1_w_gate, ffn1_w_up=ffn1_w_up, ffn1_w_down=ffn1_w_down, ln1_g=ln1_g, ln1_b=ln1_b,
             w_in=w_in, da_lambda_q1=da_lambda_q1, da_lambda_k1=da_lambda_k1, da_lambda_q2=da_lambda_q2,
             da_lambda_k2=da_lambda_k2, da_subln_g=da_subln_g, mla_q_norm_g=mla_q_norm_g, w_uq=w_uq,
             mla_kv_norm_g=mla_kv_norm_g, w_ukv=w_ukv, w_o=w_o, ln2_g=ln2_g, ln2_b=ln2_b,
             ffn2_w_gate=ffn2_w_gate, ffn2_w_up=ffn2_w_up, ffn2_w_down=ffn2_w_down, ln3_g=ln3_g, ln3_b=ln3_b)
    w = _prep_weights(p)
    s_max = max(x_prompt.shape[1], x_sample.shape[1])
    tabs_da = _rope_lane_tables(N_META + s_max, DA_ROT, DA_HEAD_DIM)
    tabs_mla = _rope_lane_tables(N_META + s_max, MLA_ROPE, LANES)

    hm = _ffn_ln(meta_tokens.astype(F32), w["ffn1_w_gate"], w["ffn1_w_up"], w["ffn1_w_down"], w["ln1_g"], w["ln1_b"])
    hm = jnp.pad(hm, ((0, PROJ_ROWS - N_META), (0, 0)))
    _, kda_m, vdat_m, _, km_m, vmt_m, knorm2_m = _mixer_inputs(
        hm, 1, PROJ_ROWS, w, [t[:PROJ_ROWS] for t in tabs_da], [t[:PROJ_ROWS] for t in tabs_mla])
    meta_kv = (kda_m[:, :, :META_PAD], vdat_m[:, :, 0, :, :META_PAD], km_m[:, :, :META_PAD],
               vmt_m[:, :, 0, :, :META_PAD], knorm2_m)

    seq_da = [t[N_META:] for t in tabs_da]
    seq_mla = [t[N_META:] for t in tabs_mla]
    y_prompt = _trunk(x_prompt, meta_kv, w, seq_da, seq_mla)
    y_sample = _trunk(x_sample, meta_kv, w, seq_da, seq_mla)
    return (y_prompt, y_sample)
```

```python
import functools
import math

import jax
import jax.numpy as jnp
from jax import lax
from jax.experimental import pallas as pl
from jax.experimental.pallas import tpu as pltpu

N_META = 16
ROPE_THETA = 500000.0
DA_HEADS = 8
DA_HEAD_DIM = 64
DA_V_DIM = 2 * DA_HEAD_DIM
DA_ROT = DA_HEAD_DIM // 4
DA_QK_COLS = DA_HEADS * 2 * DA_HEAD_DIM
DA_WIDTH = DA_HEADS * DA_V_DIM
MLA_HEADS = 8
MLA_Q_RANK = 512
MLA_KV_RANK = 256
MLA_NOPE = 128
MLA_ROPE = 64
MLA_V = 128
MLA_WIDTH = MLA_HEADS * MLA_V
MLA_QK_PAD = 256
DEPTH = 1
ALPHA = (2 * DEPTH) ** 0.25
LN_EPS = 1e-5
RMS_EPS = 1e-6
LAMBDA_INIT = 0.8 - 0.6 * math.exp(-0.3 * 0)
LOG2E = 1.4426950408889634

LANES = 128
SUBLANES = 8
META_PAD = 128
NEG_BIG = -1e30

PROJ_ROWS = 256
DA_Q_TILE = 1024
MLA_Q_TILE = 2048
KV_CHUNK = 512
CHUNKS_PER_ITER = 8
ONLINE_CHUNKS_PER_ITER = 2
OUTPROJ_SUBBLOCKS = 2
SCORE_BOUND = 45.0
BOUND_SLACK = 1.01
V_EXTRA_ROWS = 16
V_ROWS = DA_V_DIM + V_EXTRA_ROWS

F32 = jnp.float32
BF16 = jnp.bfloat16
MIB = 1024 * 1024


def _cparams(n_axes, vmem_mib):
    return pltpu.CompilerParams(
        dimension_semantics=("arbitrary",) * n_axes,
        vmem_limit_bytes=int(vmem_mib * MIB),
    )


def _resident(shape):
    n = len(shape)
    return pl.BlockSpec(shape, lambda *_: (0,) * n, pipeline_mode=pl.Buffered(1))


def _layer_norm(y, g, b):
    mu = jnp.mean(y, axis=-1, keepdims=True)
    d = y - mu
    var = jnp.mean(d * d, axis=-1, keepdims=True)
    return d * lax.rsqrt(var + LN_EPS) * g + b


def _rms_norm(x, g):
    return x * lax.rsqrt(jnp.mean(x * x, axis=-1, keepdims=True) + RMS_EPS) * g


def _ffn_ln_kernel(x_ref, wg_ref, wu_ref, wd_ref, g_ref, b_ref, o_ref, xb_ref, acc_ref, y_ref, *, n_slices):
    i, j = pl.program_id(0), pl.program_id(1)
    n_blocks, n_chunks = pl.num_programs(0) - 1, pl.num_programs(1)
    slice_rows = y_ref.shape[0] // n_slices

    def norm_slice():
        r0 = pl.multiple_of(jnp.minimum(j, n_slices - 1) * slice_rows, slice_rows)
        rows = pl.ds(r0, slice_rows)
        o_ref[rows, :] = _layer_norm(y_ref[rows, :], g_ref[...], b_ref[...])

    @pl.when(i < n_blocks)
    def _():
        @pl.when(j == 0)
        def _():
            xb_ref[...] = x_ref[...].astype(BF16)
            acc_ref[...] = jnp.zeros_like(acc_ref)

        @pl.when((i == 0) & (j == 0))
        def _():
            y_ref[...] = jnp.zeros_like(y_ref)

        norm_slice()
        xb = xb_ref[...]
        gate = jnp.dot(xb, wg_ref[...], preferred_element_type=F32)
        up = jnp.dot(xb, wu_ref[...], preferred_element_type=F32)
        hid = (gate * (1.0 / (1.0 + jnp.exp(-gate)))) * up
        acc_ref[...] += jnp.dot(hid.astype(BF16), wd_ref[...], preferred_element_type=F32)

        @pl.when(j == n_chunks - 1)
        def _():
            y_ref[...] = ALPHA * x_ref[...] + acc_ref[...]

    @pl.when(i == n_blocks)
    def _():
        norm_slice()


def _ffn_ln(x, wg, wu, wd, g, b, *, tm=512, tf=512):
    m, d = x.shape
    f = wg.shape[1]
    tm = min(tm, m)
    tf = min(tf, f)
    assert m % tm == 0 and f % tf == 0
    nb, nj = m // tm, f // tf
    n_slices = 1
    while 2 * n_slices <= min(nj, tm // SUBLANES):
        n_slices *= 2
    assert tm % (n_slices * SUBLANES) == 0

    def chunk(i, j):
        return jnp.where(i < nb, j, nj - 1)

    vmem = (2 * 2 * tm * d * 4 + 2 * 3 * d * tf * 2 + tm * d * 10 + 4 * tm * tf * 4) / MIB + 6
    return pl.pallas_call(
        functools.partial(_ffn_ln_kernel, n_slices=n_slices),
        grid=(nb + 1, nj),
        in_specs=[
            pl.BlockSpec((tm, d), lambda i, j: (jnp.minimum(i, nb - 1), 0)),
            pl.BlockSpec((d, tf), lambda i, j: (0, chunk(i, j))),
            pl.BlockSpec((d, tf), lambda i, j: (0, chunk(i, j))),
            pl.BlockSpec((tf, d), lambda i, j: (chunk(i, j), 0)),
            pl.BlockSpec((1, d), lambda i, j: (0, 0)),
            pl.BlockSpec((1, d), lambda i, j: (0, 0)),
        ],
        out_specs=pl.BlockSpec((tm, d), lambda i, j: (jnp.maximum(i - 1, 0), 0)),
        out_shape=jax.ShapeDtypeStruct((m, d), F32),
        scratch_shapes=[pltpu.VMEM((tm, d), BF16), pltpu.VMEM((tm, d), F32), pltpu.VMEM((tm, d), F32)],
        compiler_params=_cparams(2, vmem),
        name="ffn_ln",
    )(x, wg, wu, wd, g, b)


def _rope(x, c, sp, sm, half):
    return x * c + pltpu.roll(x, half, 1) * sp + pltpu.roll(x, LANES - half, 1) * sm


def _inproj_kernel(h_ref, win_ref, gq_ref, wuq_ref, gkv_ref, wukv_ref,
                   cda_ref, spda_ref, smda_ref, cm_ref, spm_ref, smm_ref,
                   qdat_ref, kda_ref, vdat_ref, qmt_ref, km_ref, vmt_ref, knorm_ref):
    tm = h_ref.shape[0]
    hb = h_ref[...].astype(BF16)

    def proj(lo, hi):
        return jnp.dot(hb, win_ref[:, lo:hi], preferred_element_type=F32)

    def row_sq_norm(x):
        return jnp.sum(x * x, axis=1, keepdims=True)

    def put_max_sq_norm(row, n2):
        knorm_ref[0, row:row + 1, :] = jnp.broadcast_to(jnp.max(n2, axis=0, keepdims=True), (1, LANES))

    cda, spda, smda = cda_ref[...], spda_ref[...], smda_ref[...]
    cm, spm, smm = cm_ref[...], spm_ref[...], smm_ref[...]
    da_half = DA_ROT // 2
    mla_half = MLA_ROPE // 2
    map1_dims = lax.broadcasted_iota(jnp.int32, (LANES, tm), 0) < DA_HEAD_DIM
    da_scale = DA_HEAD_DIM ** -0.5 * LOG2E
    mla_scale = (MLA_NOPE + MLA_ROPE) ** -0.5 * LOG2E
    sum_rows = (lax.broadcasted_iota(jnp.int32, (V_EXTRA_ROWS, tm), 0) == 0).astype(BF16)

    c0 = 2 * DA_QK_COLS + DA_WIDTH
    cq = _rms_norm(proj(c0, c0 + MLA_Q_RANK), gq_ref[...]).astype(BF16)
    c0 += MLA_Q_RANK
    qm = jnp.dot(cq, wuq_ref[...], preferred_element_type=F32) * mla_scale
    for h in range(MLA_HEADS):
        base = h * MLA_QK_PAD
        qmt_ref[0, h, 0, 0:MLA_NOPE, :] = qm[:, base:base + MLA_NOPE].T.astype(BF16)
        qmt_ref[0, h, 0, MLA_NOPE:MLA_QK_PAD, :] = _rope(
            qm[:, base + MLA_NOPE:base + MLA_QK_PAD], cm, spm, smm, mla_half).T.astype(BF16)

    ckv = _rms_norm(proj(c0, c0 + MLA_KV_RANK), gkv_ref[...]).astype(BF16)
    c0 += MLA_KV_RANK
    kv = jnp.dot(ckv, wukv_ref[...], preferred_element_type=F32)
    kr_f = _rope(proj(c0, c0 + LANES), cm, spm, smm, mla_half)
    kr = kr_f.astype(BF16)
    kr_n2 = row_sq_norm(kr_f)
    for h in range(MLA_HEADS):
        base = h * (MLA_NOPE + MLA_V)
        k_nope = kv[:, base:base + MLA_NOPE]
        km_ref[0, h, :, 0:MLA_NOPE] = k_nope.astype(BF16)
        km_ref[0, h, :, MLA_NOPE:MLA_QK_PAD] = kr
        put_max_sq_norm(DA_HEADS + h, row_sq_norm(k_nope) + kr_n2)
        vmt_ref[0, h, 0, 0:MLA_V, :] = kv[:, base + MLA_NOPE:base + MLA_NOPE + MLA_V].T.astype(BF16)
        vmt_ref[0, h, 0, MLA_V:V_ROWS, :] = sum_rows

    c0 = 0
    zq = proj(c0, c0 + DA_QK_COLS)
    for h in range(DA_HEADS):
        rt = (_rope(zq[:, h * LANES:(h + 1) * LANES], cda, spda, smda, da_half) * da_scale).T
        qdat_ref[0, h, 0, 0] = jnp.where(map1_dims, rt, 0.0).astype(BF16)
        qdat_ref[0, h, 0, 1] = jnp.where(map1_dims, 0.0, rt).astype(BF16)
    c0 += DA_QK_COLS
    zk = proj(c0, c0 + DA_QK_COLS)
    for h in range(DA_HEADS):
        k = _rope(zk[:, h * LANES:(h + 1) * LANES], cda, spda, smda, da_half)
        kda_ref[0, h] = k.astype(BF16)
        put_max_sq_norm(h, row_sq_norm(k))
    c0 += DA_QK_COLS
    zv = proj(c0, c0 + DA_WIDTH)
    for h in range(DA_HEADS):
        vdat_ref[0, h, 0, 0:DA_V_DIM, :] = zv[:, h * DA_V_DIM:(h + 1) * DA_V_DIM].T.astype(BF16)
        vdat_ref[0, h, 0, DA_V_DIM:V_ROWS, :] = sum_rows


def _inproj(h, nb, s, win, gq, wuq, gkv, wukv, tabs_da, tabs_mla):
    m, d = h.shape
    tm = PROJ_ROWS
    ck = min(KV_CHUNK, s)
    tqd = min(DA_Q_TILE, s)
    tqm = min(MLA_Q_TILE, s)
    assert m == nb * s and all(t % tm == 0 and s % t == 0 for t in (tm, ck, tqd, tqm))
    spb = s // tm
    tab_spec = pl.BlockSpec((tm, LANES), lambda i: (i % spb, 0))

    def rows_spec(width):
        return pl.BlockSpec((1, DA_HEADS, tm, width), lambda i: (i // spb, 0, i % spb, 0))

    def cols_spec(rows, tile, *mid):
        per = tile // tm
        zeros = (0,) * (len(mid) + 1)
        return pl.BlockSpec((1, DA_HEADS, 1) + mid + (rows, tm),
                            lambda i: (i // spb, 0, (i % spb) // per) + zeros + ((i % spb) % per,))

    def shape(*tail):
        return jax.ShapeDtypeStruct((nb, DA_HEADS) + tail, BF16)

    vmem = (2 * tm * d * 4 + (win.size + wuq.size + wukv.size) * 2 + 2 * 6 * tm * LANES * 4
            + 2 * tm * 8 * (256 + 128 + 128 + 256 + 256 + 128) * 2 + 8 * tm * 2048 * 4) / MIB + 6
    return pl.pallas_call(
        _inproj_kernel,
        grid=(m // tm,),
        in_specs=[
            pl.BlockSpec((tm, d), lambda i: (i, 0)),
            _resident(win.shape), _resident(gq.shape), _resident(wuq.shape),
            _resident(gkv.shape), _resident(wukv.shape),
            tab_spec, tab_spec, tab_spec, tab_spec, tab_spec, tab_spec,
        ],
        out_specs=[
            cols_spec(LANES, tqd, 2),
            rows_spec(LANES), cols_spec(V_ROWS, ck),
            cols_spec(MLA_QK_PAD, tqm), rows_spec(MLA_QK_PAD), cols_spec(V_ROWS, ck),
            pl.BlockSpec((1, DA_HEADS + MLA_HEADS, LANES), lambda i: (i, 0, 0)),
        ],
        out_shape=[
            shape(s // tqd, 2, LANES, tqd), shape(s, LANES), shape(s // ck, V_ROWS, ck),
            shape(s // tqm, MLA_QK_PAD, tqm), shape(s, MLA_QK_PAD), shape(s // ck, V_ROWS, ck),
            jax.ShapeDtypeStruct((m // tm, DA_HEADS + MLA_HEADS, LANES), F32),
        ],
        compiler_params=_cparams(1, vmem),
        name="inproj",
    )(h, win, gq, wuq, gkv, wukv, *tabs_da, *tabs_mla)


def _masked_meta_scores(kmeta, qt):
    s = jnp.dot(kmeta, qt, preferred_element_type=F32)
    return jnp.where(lax.broadcasted_iota(jnp.int32, s.shape, 0) < N_META, s, NEG_BIG)


def _score_shift(qt, kmax):
    qf = qt.astype(F32)
    return jnp.sqrt(jnp.sum(qf * qf, axis=0, keepdims=True)) * (kmax[0:1, 0:1] * BOUND_SLACK)


def _flash_cols_shifted(qt, s_meta, shift, k_ref, vt_ref, vmeta_t, acc_scr):
    n_chunks, tk = vt_ref.shape[2], vt_ref.shape[4]
    group = CHUNKS_PER_ITER

    def weights(s):
        return jnp.exp2(s - shift).astype(BF16)

    def chunk_pv(c):
        off = c * tk if isinstance(c, int) else pl.multiple_of(c * tk, tk)
        s = jnp.dot(k_ref[0, 0, pl.ds(off, tk), :], qt, preferred_element_type=F32)
        return jnp.dot(vt_ref[0, 0, c], weights(s), preferred_element_type=F32)

    acc_scr[...] = jnp.zeros_like(acc_scr)

    def body(j, carry):
        total = chunk_pv(group * j)
        for u in range(1, group):
            total = total + chunk_pv(group * j + u)
        acc_scr[...] += total
        return carry

    n_iter = n_chunks // group if n_chunks >= 2 * group else 0
    lax.fori_loop(0, n_iter, body, 0)
    for c in range(group * n_iter, n_chunks):
        acc_scr[...] += chunk_pv(c)
    return acc_scr[...] + jnp.dot(vmeta_t, weights(s_meta), preferred_element_type=F32)


def _flash_cols_online(qt, s_meta, k_ref, vt_ref, vmeta_t, s_scrs, cmax_scrs, m_scr, acc_scr):
    n_chunks, tk = vt_ref.shape[2], vt_ref.shape[4]
    r = qt.shape[1]
    group = ONLINE_CHUNKS_PER_ITER
    assert group % 2 == 0

    def bcast(x):
        return jnp.broadcast_to(x, (SUBLANES, r))

    def scores(c):
        off = c * tk if isinstance(c, int) else pl.multiple_of(c * tk, tk)
        return jnp.dot(k_ref[0, 0, pl.ds(off, tk), :], qt, preferred_element_type=F32)

    def put(s, slot, rows):
        s_scrs[slot][0:rows, :] = s
        cmax_scrs[slot][...] = bcast(jnp.max(s, axis=0, keepdims=True))

    def fold(slot, rows, vt, first=False):
        cmax = cmax_scrs[slot][0:1, :]
        m_new = cmax if first else jnp.maximum(m_scr[0:1, :], cmax)
        p = jnp.exp2(s_scrs[slot][0:rows, :] - m_new)
        pv = jnp.dot(vt, p.astype(BF16), preferred_element_type=F32)
        if first:
            acc_scr[...] = pv
        else:
            acc_scr[...] = jnp.exp2(m_scr[0:1, :] - m_new) * acc_scr[...] + pv
        m_scr[...] = bcast(m_new)

    put(s_meta, 1, META_PAD)
    put(scores(0), 0, tk)
    fold(1, META_PAD, vmeta_t, first=True)

    def step(c, parity, last):
        if not last:
            put(scores(c + 1), 1 - parity, tk)
        fold(parity, tk, vt_ref[0, 0, c])

    n_iter = (n_chunks - 1) // group

    def body(j, carry):
        for u in range(group):
            step(group * j + u, u % 2, False)
        return carry

    lax.fori_loop(0, n_iter, body, 0)
    for c in range(group * n_iter, n_chunks):
        step(c, c % 2, c == n_chunks - 1)
    return acc_scr[...]


def _normalised(acc, dv):
    return acc[0:dv, :] * (1.0 / acc[dv:dv + 1, :])


def _flash_cols(qt, k_ref, vt_ref, kmeta, vmeta_t, kmax, finish,
                s0_scr, s1_scr, c0_scr, c1_scr, m_scr, acc_scr):
    s_meta = _masked_meta_scores(kmeta, qt)
    shift = _score_shift(qt, kmax)
    bounded = jnp.max(shift) <= SCORE_BOUND

    @pl.when(bounded)
    def _():
        finish(_flash_cols_shifted(qt, s_meta, shift, k_ref, vt_ref, vmeta_t, acc_scr))

    @pl.when(jnp.logical_not(bounded))
    def _():
        finish(_flash_cols_online(qt, s_meta, k_ref, vt_ref, vmeta_t,
                                  (s0_scr, s1_scr), (c0_scr, c1_scr), m_scr, acc_scr))


def _da_attn_kernel(qt_ref, k_ref, vt_ref, kmeta_ref, vmeta_ref, kmax_ref, lq1_ref, lk1_ref, lq2_ref, lk2_ref,
                    g_ref, o_ref, *scratch):
    tq = o_ref.shape[1]
    qt = jnp.concatenate([qt_ref[0, 0, 0, 0], qt_ref[0, 0, 0, 1]], axis=1)

    def finish(acc):
        at = _normalised(acc, DA_V_DIM)
        lam = (jnp.exp(jnp.sum(lq1_ref[...] * lk1_ref[...], axis=1, keepdims=True))
               - jnp.exp(jnp.sum(lq2_ref[...] * lk2_ref[...], axis=1, keepdims=True)) + LAMBDA_INIT)
        ot = at[:, :tq] - lam * at[:, tq:]
        inv = lax.rsqrt(jnp.mean(ot * ot, axis=0, keepdims=True) + RMS_EPS)
        o_ref[0] = ((ot * inv).T * g_ref[...] * (1.0 - LAMBDA_INIT)).astype(BF16)

    _flash_cols(qt, k_ref, vt_ref, kmeta_ref[0, 0], vmeta_ref[0, 0], kmax_ref[0, 0], finish, *scratch)


def _mla_attn_kernel(qt_ref, k_ref, vt_ref, kmeta_ref, vmeta_ref, kmax_ref, o_ref, *scratch):
    def finish(acc):
        o_ref[0] = _normalised(acc, MLA_V).T.astype(BF16)

    _flash_cols(qt_ref[0, 0, 0], k_ref, vt_ref, kmeta_ref[0, 0], vmeta_ref[0, 0], kmax_ref[0, 0], finish, *scratch)


def _attn_scratch(cols, tk):
    stat = pltpu.VMEM((SUBLANES, cols), F32)
    return [pltpu.VMEM((tk, cols), F32), pltpu.VMEM((tk, cols), F32), stat, stat, stat,
            pltpu.VMEM((V_ROWS, cols), F32)]


def _kmax_spec():
    return pl.BlockSpec((1, 1, 1, LANES), lambda b, h, i: (b, h, 0, 0))


def _attn_vmem(cols, tk, s, dk, dv):
    return (2 * s * (dk + dv) * 2 + 2 * dk * cols * 2 + 2 * tk * cols * 4 + dv * cols * 4
            + 4 * tk * cols * 4 + 2 * cols * dv * 2) / MIB + 8


def _da_attn(qdat, kda, vdat, kmeta, vmeta_t, kmax, lam_vecs, g):
    nb, nh, nq, _, _, tq = qdat.shape
    cols = 2 * tq
    s = kda.shape[2]
    nc, tk = vdat.shape[2], vdat.shape[4]
    vec_spec = pl.BlockSpec((1, DA_HEAD_DIM), lambda b, h, i: (0, 0))
    meta_spec = pl.BlockSpec((1, 1, META_PAD, LANES), lambda b, h, i: (0, h, 0, 0))
    vmeta_spec = pl.BlockSpec((1, 1, V_ROWS, META_PAD), lambda b, h, i: (0, h, 0, 0))
    return pl.pallas_call(
        _da_attn_kernel,
        grid=(nb, nh, nq),
        in_specs=[
            pl.BlockSpec((1, 1, 1, 2, LANES, tq), lambda b, h, i: (b, h, i, 0, 0, 0)),
            pl.BlockSpec((1, 1, s, LANES), lambda b, h, i: (b, h, 0, 0)),
            pl.BlockSpec((1, 1, nc, V_ROWS, tk), lambda b, h, i: (b, h, 0, 0, 0)),
            meta_spec, vmeta_spec, _kmax_spec(),
            vec_spec, vec_spec, vec_spec, vec_spec,
            pl.BlockSpec((1, DA_V_DIM), lambda b, h, i: (0, 0)),
        ],
        out_specs=pl.BlockSpec((1, tq, DA_V_DIM), lambda b, h, i: (b, i, h)),
        out_shape=jax.ShapeDtypeStruct((nb, s, DA_WIDTH), BF16),
        scratch_shapes=_attn_scratch(cols, tk),
        compiler_params=_cparams(3, _attn_vmem(cols, tk, s, LANES, DA_V_DIM)),
        name="da_attn",
    )(qdat, kda, vdat, kmeta, vmeta_t, kmax, *lam_vecs, g)


def _mla_attn(qmt, km, vmt, kmeta, vmeta_t, kmax):
    nb, nh, nq, _, tq = qmt.shape
    s = km.shape[2]
    nc, tk = vmt.shape[2], vmt.shape[4]
    return pl.pallas_call(
        _mla_attn_kernel,
        grid=(nb, nh, nq),
        in_specs=[
            pl.BlockSpec((1, 1, 1, MLA_QK_PAD, tq), lambda b, h, i: (b, h, i, 0, 0)),
            pl.BlockSpec((1, 1, s, MLA_QK_PAD), lambda b, h, i: (b, h, 0, 0)),
            pl.BlockSpec((1, 1, nc, V_ROWS, tk), lambda b, h, i: (b, h, 0, 0, 0)),
            pl.BlockSpec((1, 1, META_PAD, MLA_QK_PAD), lambda b, h, i: (0, h, 0, 0)),
            pl.BlockSpec((1, 1, V_ROWS, META_PAD), lambda b, h, i: (0, h, 0, 0)),
            _kmax_spec(),
        ],
        out_specs=pl.BlockSpec((1, tq, MLA_V), lambda b, h, i: (b, i, h)),
        out_shape=jax.ShapeDtypeStruct((nb, s, MLA_WIDTH), BF16),
        scratch_shapes=_attn_scratch(tq, tk),
        compiler_params=_cparams(3, _attn_vmem(tq, tk, s, MLA_QK_PAD, MLA_V)),
        name="mla_attn",
    )(qmt, km, vmt, kmeta, vmeta_t, kmax)


def _outproj_ln_kernel(h_ref, oa_ref, om_ref, wo_ref, g_ref, b_ref, o_ref):
    tm = h_ref.shape[0]
    sub = tm // OUTPROJ_SUBBLOCKS if tm % (OUTPROJ_SUBBLOCKS * 16) == 0 else tm
    for r0 in range(0, tm, sub):
        rows = pl.ds(r0, sub)
        mix = (jnp.dot(oa_ref[rows, :], wo_ref[0:DA_WIDTH, :], preferred_element_type=F32)
               + jnp.dot(om_ref[rows, :], wo_ref[DA_WIDTH:DA_WIDTH + MLA_WIDTH, :], preferred_element_type=F32))
        o_ref[rows, :] = _layer_norm(ALPHA * h_ref[rows, :] + mix, g_ref[...], b_ref[...])


def _outproj_ln(h, oa, om, wo, g, b, *, tm=512):
    m, d = h.shape
    tm = min(tm, m)
    assert m % tm == 0
    vmem = (2 * 2 * tm * d * 4 + 2 * 2 * tm * DA_WIDTH * 2 + wo.size * 2 + 3 * tm * d * 4) / MIB + 6
    return pl.pallas_call(
        _outproj_ln_kernel,
        grid=(m // tm,),
        in_specs=[
            pl.BlockSpec((tm, d), lambda i: (i, 0)),
            pl.BlockSpec((tm, DA_WIDTH), lambda i: (i, 0)),
            pl.BlockSpec((tm, MLA_WIDTH), lambda i: (i, 0)),
            _resident(wo.shape), _resident(g.shape), _resident(b.shape),
        ],
        out_specs=pl.BlockSpec((tm, d), lambda i: (i, 0)),
        out_shape=jax.ShapeDtypeStruct((m, d), F32),
        compiler_params=_cparams(1, vmem),
        name="outproj_ln",
    )(h, oa, om, wo, g, b)


def _rope_lane_tables(n_pos, rot_dim, period):
    half = rot_dim // 2
    inv = 1.0 / (ROPE_THETA ** (jnp.arange(0, rot_dim, 2, dtype=F32) / rot_dim))
    ang = jnp.arange(n_pos, dtype=F32)[:, None] * inv[None, :]
    cos, sin = jnp.cos(ang), jnp.sin(ang)
    within = jnp.arange(LANES) % period
    idx = within % half
    c = jnp.where(within < rot_dim, cos[:, idx], 1.0)
    sp = jnp.where((within >= half) & (within < rot_dim), sin[:, idx], 0.0)
    sm = jnp.where(within < half, -sin[:, idx], 0.0)
    return c, sp, sm


def _prep_weights(p):
    w = {}
    for name in ("ffn1_w_gate", "ffn1_w_up", "ffn2_w_gate", "ffn2_w_up", "w_o", "w_ukv"):
        w[name] = p[name][0].astype(BF16)
    for name in ("ffn1_w_down", "ffn2_w_down"):
        w[name] = (p[name][0] * 0.5).astype(BF16)
    w_in = p["w_in"][0]
    w["w_in"] = jnp.pad(w_in, ((0, 0), (0, LANES - MLA_ROPE))).astype(BF16)
    w_uq = p["w_uq"][0].reshape(MLA_Q_RANK, MLA_HEADS, MLA_NOPE + MLA_ROPE)
    w_uq = jnp.pad(w_uq, ((0, 0), (0, 0), (0, MLA_QK_PAD - MLA_NOPE - MLA_ROPE)))
    w["w_uq"] = w_uq.reshape(MLA_Q_RANK, MLA_HEADS * MLA_QK_PAD).astype(BF16)
    for name in ("ln1_g", "ln1_b", "ln2_g", "ln2_b", "ln3_g", "ln3_b", "da_subln_g", "mla_q_norm_g", "mla_kv_norm_g",
                 "da_lambda_q1", "da_lambda_k1", "da_lambda_q2", "da_lambda_k2"):
        w[name] = p[name].astype(F32)
    return w


def _mixer_inputs(h, nb, s, w, tabs_da, tabs_mla):
    return _inproj(h, nb, s, w["w_in"], w["mla_q_norm_g"], w["w_uq"], w["mla_kv_norm_g"], w["w_ukv"],
                   tabs_da, tabs_mla)


def _trunk(x, meta_kv, w, tabs_da, tabs_mla):
    nb, s, d = x.shape
    h1 = _ffn_ln(x.reshape(nb * s, d), w["ffn1_w_gate"], w["ffn1_w_up"], w["ffn1_w_down"], w["ln1_g"], w["ln1_b"])
    qdat, kda, vdat, qmt, km, vmt, knorm2 = _mixer_inputs(h1, nb, s, w, tabs_da, tabs_mla)
    kda_m, vdat_m, km_m, vmt_m, knorm2_m = meta_kv
    knorm2 = jnp.max(knorm2.reshape(nb, -1, DA_HEADS + MLA_HEADS, LANES), axis=1)
    kmax = jnp.sqrt(jnp.maximum(knorm2, knorm2_m))[:, :, None, :]
    lam_vecs = (w["da_lambda_q1"], w["da_lambda_k1"], w["da_lambda_q2"], w["da_lambda_k2"])
    oa = _da_attn(qdat, kda, vdat, kda_m, vdat_m, kmax[:, :DA_HEADS], lam_vecs, w["da_subln_g"])
    om = _mla_attn(qmt, km, vmt, km_m, vmt_m, kmax[:, DA_HEADS:])
    h2 = _outproj_ln(h1, oa.reshape(nb * s, DA_WIDTH), om.reshape(nb * s, MLA_WIDTH), w["w_o"], w["ln2_g"], w["ln2_b"])
    y = _ffn_ln(h2, w["ffn2_w_gate"], w["ffn2_w_up"], w["ffn2_w_down"], w["ln3_g"], w["ln3_b"])
    return y.reshape(nb, s, d)


def kernel(x_prompt, x_sample, meta_tokens, ffn1_w_gate, ffn1_w_up, ffn1_w_down, ln1_g, ln1_b, w_in, da_lambda_q1, da_lambda_k1, da_lambda_q2, da_lambda_k2, da_subln_g, mla_q_norm_g, w_uq, mla_kv_norm_g, w_ukv, w_o, ln2_g, ln2_b, ffn2_w_gate, ffn2_w_up, ffn2_w_down, ln3_g, ln3_b):
    p = dict(ffn1_w_gate=ffn1_w_gate, ffn1_w_up=ffn1_w_up, ffn1_w_down=ffn1_w_down, ln1_g=ln1_g, ln1_b=ln1_b,
             w_in=w_in, da_lambda_q1=da_lambda_q1, da_lambda_k1=da_lambda_k1, da_lambda_q2=da_lambda_q2,
             da_lambda_k2=da_lambda_k2, da_subln_g=da_subln_g, mla_q_norm_g=mla_q_norm_g, w_uq=w_uq,
             mla_kv_norm_g=mla_kv_norm_g, w_ukv=w_ukv, w_o=w_o, ln2_g=ln2_g, ln2_b=ln2_b,
             ffn2_w_gate=ffn2_w_gate, ffn2_w_up=ffn2_w_up, ffn2_w_down=ffn2_w_down, ln3_g=ln3_g, ln3_b=ln3_b)
    w = _prep_weights(p)
    s_max = max(x_prompt.shape[1], x_sample.shape[1])
    tabs_da = _rope_lane_tables(N_META + s_max, DA_ROT, DA_HEAD_DIM)
    tabs_mla = _rope_lane_tables(N_META + s_max, MLA_ROPE, LANES)

    hm = _ffn_ln(meta_tokens.astype(F32), w["ffn1_w_gate"], w["ffn1_w_up"], w["ffn1_w_down"], w["ln1_g"], w["ln1_b"])
    hm = jnp.pad(hm, ((0, PROJ_ROWS - N_META), (0, 0)))
    _, kda_m, vdat_m, _, km_m, vmt_m, knorm2_m = _mixer_inputs(
        hm, 1, PROJ_ROWS, w, [t[:PROJ_ROWS] for t in tabs_da], [t[:PROJ_ROWS] for t in tabs_mla])
    meta_kv = (kda_m[:, :, :META_PAD], vdat_m[:, :, 0, :, :META_PAD], km_m[:, :, :META_PAD],
               vmt_m[:, :, 0, :, :META_PAD], knorm2_m)

    seq_da = [t[N_META:] for t in tabs_da]
    seq_mla = [t[N_META:] for t in tabs_mla]
    y_prompt = _trunk(x_prompt, meta_kv, w, seq_da, seq_mla)
    y_sample = _trunk(x_sample, meta_kv, w, seq_da, seq_mla)
    return (y_prompt, y_sample)
```

```python
import functools
import math

import jax
import jax.numpy as jnp
from jax import lax
from jax.experimental import pallas as pl
from jax.experimental.pallas import tpu as pltpu

N_META = 16
ROPE_THETA = 500000.0
DA_HEADS = 8
DA_HEAD_DIM = 64
DA_V_DIM = 2 * DA_HEAD_DIM
DA_ROT = DA_HEAD_DIM // 4
DA_QK_COLS = DA_HEADS * 2 * DA_HEAD_DIM
DA_WIDTH = DA_HEADS * DA_V_DIM
MLA_HEADS = 8
MLA_Q_RANK = 512
MLA_KV_RANK = 256
MLA_NOPE = 128
MLA_ROPE = 64
MLA_V = 128
MLA_WIDTH = MLA_HEADS * MLA_V
MLA_QK_PAD = 256
DEPTH = 1
ALPHA = (2 * DEPTH) ** 0.25
LN_EPS = 1e-5
RMS_EPS = 1e-6
LAMBDA_INIT = 0.8 - 0.6 * math.exp(-0.3 * 0)
LOG2E = 1.4426950408889634

LANES = 128
SUBLANES = 8
BF16_SUBLANES = 16
META_PAD = 128
NEG_BIG = -1e30

PROJ_ROWS = 256
DA_Q_TILE = 1024
MLA_Q_TILE = 2048
KV_CHUNK = 512
CHUNKS_PER_ITER = 8
ONLINE_CHUNKS_PER_ITER = 2
OUTPROJ_SUBBLOCKS = 2
SCORE_BOUND = 45.0
BOUND_SLACK = 1.01
V_EXTRA_ROWS = BF16_SUBLANES
V_ROWS = DA_V_DIM + V_EXTRA_ROWS

F32 = jnp.float32
BF16 = jnp.bfloat16
MIB = 1024 * 1024


def _cparams(n_axes, vmem_mib):
    return pltpu.CompilerParams(
        dimension_semantics=("arbitrary",) * n_axes,
        vmem_limit_bytes=int(vmem_mib * MIB),
    )


def _resident(shape):
    n = len(shape)
    return pl.BlockSpec(shape, lambda *_: (0,) * n, pipeline_mode=pl.Buffered(1))


def _layer_norm(y, g, b):
    mu = jnp.mean(y, axis=-1, keepdims=True)
    d = y - mu
    var = jnp.mean(d * d, axis=-1, keepdims=True)
    return d * lax.rsqrt(var + LN_EPS) * g + b


def _rms_norm(x, g):
    return x * lax.rsqrt(jnp.mean(x * x, axis=-1, keepdims=True) + RMS_EPS) * g


def _ffn_ln_kernel(x_ref, wg_ref, wu_ref, wd_ref, g_ref, b_ref, o_ref, xb_ref, acc_ref, y_ref, *, n_slices):
    i, j = pl.program_id(0), pl.program_id(1)
    n_blocks, n_chunks = pl.num_programs(0) - 1, pl.num_programs(1)
    slice_rows = y_ref.shape[0] // n_slices

    def norm_slice():
        r0 = pl.multiple_of(jnp.minimum(j, n_slices - 1) * slice_rows, slice_rows)
        rows = pl.ds(r0, slice_rows)
        o_ref[rows, :] = _layer_norm(y_ref[rows, :], g_ref[...], b_ref[...])

    @pl.when(i < n_blocks)
    def _():
        @pl.when(j == 0)
        def _():
            xb_ref[...] = x_ref[...].astype(BF16)
            acc_ref[...] = jnp.zeros_like(acc_ref)

        @pl.when((i == 0) & (j == 0))
        def _():
            y_ref[...] = jnp.zeros_like(y_ref)

        norm_slice()
        xb = xb_ref[...]
        gate = jnp.dot(xb, wg_ref[...], preferred_element_type=F32)
        up = jnp.dot(xb, wu_ref[...], preferred_element_type=F32)
        hid = (gate * (1.0 / (1.0 + jnp.exp(-gate)))) * up
        acc_ref[...] += jnp.dot(hid.astype(BF16), wd_ref[...], preferred_element_type=F32)

        @pl.when(j == n_chunks - 1)
        def _():
            y_ref[...] = ALPHA * x_ref[...] + acc_ref[...]

    @pl.when(i == n_blocks)
    def _():
        norm_slice()


def _ffn_ln(x, wg, wu, wd, g, b, *, tm=512, tf=512):
    m, d = x.shape
    f = wg.shape[1]
    tm = min(tm, m)
    tf = min(tf, f)
    assert m % tm == 0 and f % tf == 0
    nb, nj = m // tm, f // tf
    n_slices = 1
    while 2 * n_slices <= min(nj, tm // SUBLANES):
        n_slices *= 2
    assert tm % (n_slices * SUBLANES) == 0

    def chunk(i, j):
        return jnp.where(i < nb, j, nj - 1)

    vmem = (2 * 2 * tm * d * 4 + 2 * 3 * d * tf * 2 + tm * d * 10 + 4 * tm * tf * 4) / MIB + 6
    return pl.pallas_call(
        functools.partial(_ffn_ln_kernel, n_slices=n_slices),
        grid=(nb + 1, nj),
        in_specs=[
            pl.BlockSpec((tm, d), lambda i, j: (jnp.minimum(i, nb - 1), 0)),
            pl.BlockSpec((d, tf), lambda i, j: (0, chunk(i, j))),
            pl.BlockSpec((d, tf), lambda i, j: (0, chunk(i, j))),
            pl.BlockSpec((tf, d), lambda i, j: (chunk(i, j), 0)),
            pl.BlockSpec((1, d), lambda i, j: (0, 0)),
            pl.BlockSpec((1, d), lambda i, j: (0, 0)),
        ],
        out_specs=pl.BlockSpec((tm, d), lambda i, j: (jnp.maximum(i - 1, 0), 0)),
        out_shape=jax.ShapeDtypeStruct((m, d), F32),
        scratch_shapes=[pltpu.VMEM((tm, d), BF16), pltpu.VMEM((tm, d), F32), pltpu.VMEM((tm, d), F32)],
        compiler_params=_cparams(2, vmem),
        name="ffn_ln",
    )(x, wg, wu, wd, g, b)


def _rope(x, c, sp, sm, half):
    return x * c + pltpu.roll(x, half, 1) * sp + pltpu.roll(x, LANES - half, 1) * sm


def _inproj_kernel(h_ref, win_ref, gq_ref, wuq_ref, gkv_ref, wukv_ref,
                   cda_ref, spda_ref, smda_ref, cm_ref, spm_ref, smm_ref,
                   qdat_ref, kda_ref, vdat_ref, qmt_ref, km_ref, vmt_ref, knorm_ref):
    tm = h_ref.shape[0]
    hb = h_ref[...].astype(BF16)

    def proj(lo, hi):
        return jnp.dot(hb, win_ref[:, lo:hi], preferred_element_type=F32)

    def row_sq_norm(x):
        return jnp.sum(x * x, axis=1, keepdims=True)

    def put_max_sq_norm(row, n2):
        knorm_ref[0, row:row + 1, :] = jnp.broadcast_to(jnp.max(n2, axis=0, keepdims=True), (1, LANES))

    cda, spda, smda = cda_ref[...], spda_ref[...], smda_ref[...]
    cm, spm, smm = cm_ref[...], spm_ref[...], smm_ref[...]
    da_half = DA_ROT // 2
    mla_half = MLA_ROPE // 2
    map1_dims = lax.broadcasted_iota(jnp.int32, (LANES, tm), 0) < DA_HEAD_DIM
    da_scale = DA_HEAD_DIM ** -0.5 * LOG2E
    mla_scale = (MLA_NOPE + MLA_ROPE) ** -0.5 * LOG2E
    sum_rows = (lax.broadcasted_iota(jnp.int32, (V_EXTRA_ROWS, tm), 0) == 0).astype(BF16)

    c0 = 2 * DA_QK_COLS + DA_WIDTH
    cq = _rms_norm(proj(c0, c0 + MLA_Q_RANK), gq_ref[...]).astype(BF16)
    c0 += MLA_Q_RANK
    qm = jnp.dot(cq, wuq_ref[...], preferred_element_type=F32) * mla_scale
    for h in range(MLA_HEADS):
        base = h * MLA_QK_PAD
        qmt_ref[0, h, 0, 0:MLA_NOPE, :] = qm[:, base:base + MLA_NOPE].T.astype(BF16)
        qmt_ref[0, h, 0, MLA_NOPE:MLA_QK_PAD, :] = _rope(
            qm[:, base + MLA_NOPE:base + MLA_QK_PAD], cm, spm, smm, mla_half).T.astype(BF16)

    ckv = _rms_norm(proj(c0, c0 + MLA_KV_RANK), gkv_ref[...]).astype(BF16)
    c0 += MLA_KV_RANK
    kv = jnp.dot(ckv, wukv_ref[...], preferred_element_type=F32)
    kr_f = _rope(proj(c0, c0 + LANES), cm, spm, smm, mla_half)
    kr = kr_f.astype(BF16)
    kr_n2 = row_sq_norm(kr_f)
    for h in range(MLA_HEADS):
        base = h * (MLA_NOPE + MLA_V)
        k_nope = kv[:, base:base + MLA_NOPE]
        km_ref[0, h, :, 0:MLA_NOPE] = k_nope.astype(BF16)
        km_ref[0, h, :, MLA_NOPE:MLA_QK_PAD] = kr
        put_max_sq_norm(DA_HEADS + h, row_sq_norm(k_nope) + kr_n2)
        vmt_ref[0, h, 0, 0:MLA_V, :] = kv[:, base + MLA_NOPE:base + MLA_NOPE + MLA_V].T.astype(BF16)
        vmt_ref[0, h, 0, MLA_V:V_ROWS, :] = sum_rows

    c0 = 0
    zq = proj(c0, c0 + DA_QK_COLS)
    for h in range(DA_HEADS):
        rt = (_rope(zq[:, h * LANES:(h + 1) * LANES], cda, spda, smda, da_half) * da_scale).T
        qdat_ref[0, h, 0, 0] = jnp.where(map1_dims, rt, 0.0).astype(BF16)
        qdat_ref[0, h, 0, 1] = jnp.where(map1_dims, 0.0, rt).astype(BF16)
    c0 += DA_QK_COLS
    zk = proj(c0, c0 + DA_QK_COLS)
    for h in range(DA_HEADS):
        k = _rope(zk[:, h * LANES:(h + 1) * LANES], cda, spda, smda, da_half)
        kda_ref[0, h] = k.astype(BF16)
        put_max_sq_norm(h, row_sq_norm(k))
    c0 += DA_QK_COLS
    zv = proj(c0, c0 + DA_WIDTH)
    for h in range(DA_HEADS):
        vdat_ref[0, h, 0, 0:DA_V_DIM, :] = zv[:, h * DA_V_DIM:(h + 1) * DA_V_DIM].T.astype(BF16)
        vdat_ref[0, h, 0, DA_V_DIM:V_ROWS, :] = sum_rows


def _inproj(h, nb, s, win, gq, wuq, gkv, wukv, tabs_da, tabs_mla):
    m, d = h.shape
    tm = PROJ_ROWS
    ck = min(KV_CHUNK, s)
    tqd = min(DA_Q_TILE, s)
    tqm = min(MLA_Q_TILE, s)
    assert m == nb * s and all(t % tm == 0 and s % t == 0 for t in (tm, ck, tqd, tqm))
    spb = s // tm
    tab_spec = pl.BlockSpec((tm, LANES), lambda i: (i % spb, 0))

    def rows_spec(width):
        return pl.BlockSpec((1, DA_HEADS, tm, width), lambda i: (i // spb, 0, i % spb, 0))

    def cols_spec(rows, tile, *mid):
        per = tile // tm
        zeros = (0,) * (len(mid) + 1)
        return pl.BlockSpec((1, DA_HEADS, 1) + mid + (rows, tm),
                            lambda i: (i // spb, 0, (i % spb) // per) + zeros + ((i % spb) % per,))

    def shape(*tail):
        return jax.ShapeDtypeStruct((nb, DA_HEADS) + tail, BF16)

    vmem = (2 * tm * d * 4 + (win.size + wuq.size + wukv.size) * 2 + 2 * 6 * tm * LANES * 4
            + 2 * tm * 8 * (256 + 128 + 128 + 256 + 256 + 128) * 2 + 8 * tm * 2048 * 4) / MIB + 6
    return pl.pallas_call(
        _inproj_kernel,
        grid=(m // tm,),
        in_specs=[
            pl.BlockSpec((tm, d), lambda i: (i, 0)),
            _resident(win.shape), _resident(gq.shape), _resident(wuq.shape),
            _resident(gkv.shape), _resident(wukv.shape),
            tab_spec, tab_spec, tab_spec, tab_spec, tab_spec, tab_spec,
        ],
        out_specs=[
            cols_spec(LANES, tqd, 2),
            rows_spec(LANES), cols_spec(V_ROWS, ck),
            cols_spec(MLA_QK_PAD, tqm), rows_spec(MLA_QK_PAD), cols_spec(V_ROWS, ck),
            pl.BlockSpec((1, DA_HEADS + MLA_HEADS, LANES), lambda i: (i, 0, 0)),
        ],
        out_shape=[
            shape(s // tqd, 2, LANES, tqd), shape(s, LANES), shape(s // ck, V_ROWS, ck),
            shape(s // tqm, MLA_QK_PAD, tqm), shape(s, MLA_QK_PAD), shape(s // ck, V_ROWS, ck),
            jax.ShapeDtypeStruct((m // tm, DA_HEADS + MLA_HEADS, LANES), F32),
        ],
        compiler_params=_cparams(1, vmem),
        name="inproj",
    )(h, win, gq, wuq, gkv, wukv, *tabs_da, *tabs_mla)


def _masked_meta_scores(kmeta, qt):
    s = jnp.dot(kmeta, qt, preferred_element_type=F32)
    return jnp.where(lax.broadcasted_iota(jnp.int32, s.shape, 0) < N_META, s, NEG_BIG)


def _score_shift(qt, kmax):
    qf = qt.astype(F32)
    return jnp.sqrt(jnp.sum(qf * qf, axis=0, keepdims=True)) * (kmax[0:1, 0:1] * BOUND_SLACK)


def _flash_cols_shifted(qt, s_meta, shift, k_ref, vt_ref, vmeta_t, acc_scr):
    n_chunks, tk = vt_ref.shape[2], vt_ref.shape[4]
    group = CHUNKS_PER_ITER

    def weights(s):
        return jnp.exp2(s - shift).astype(BF16)

    def chunk_pv(c):
        off = c * tk if isinstance(c, int) else pl.multiple_of(c * tk, tk)
        s = jnp.dot(k_ref[0, 0, pl.ds(off, tk), :], qt, preferred_element_type=F32)
        return jnp.dot(vt_ref[0, 0, c], weights(s), preferred_element_type=F32)

    def body(j, carry):
        total = chunk_pv(group * j)
        for u in range(1, group):
            total = total + chunk_pv(group * j + u)
        acc_scr[...] += total
        return carry

    n_iter = n_chunks // group if n_chunks >= 2 * group else 0
    lax.fori_loop(0, n_iter, body, 0)
    for c in range(group * n_iter, n_chunks):
        acc_scr[...] += chunk_pv(c)
    return acc_scr[...] + jnp.dot(vmeta_t, weights(s_meta), preferred_element_type=F32)


def _flash_cols_online(qt, s_meta, k_ref, vt_ref, vmeta_t, s_scrs, cmax_scrs, m_scr, acc_scr):
    n_chunks, tk = vt_ref.shape[2], vt_ref.shape[4]
    r = qt.shape[1]
    group = ONLINE_CHUNKS_PER_ITER
    assert group % 2 == 0

    def bcast(x):
        return jnp.broadcast_to(x, (SUBLANES, r))

    def scores(c):
        off = c * tk if isinstance(c, int) else pl.multiple_of(c * tk, tk)
        return jnp.dot(k_ref[0, 0, pl.ds(off, tk), :], qt, preferred_element_type=F32)

    def put(s, slot, rows):
        s_scrs[slot][0:rows, :] = s
        cmax_scrs[slot][...] = bcast(jnp.max(s, axis=0, keepdims=True))

    def fold(slot, rows, vt, first=False):
        cmax = cmax_scrs[slot][0:1, :]
        m_new = cmax if first else jnp.maximum(m_scr[0:1, :], cmax)
        p = jnp.exp2(s_scrs[slot][0:rows, :] - m_new)
        pv = jnp.dot(vt, p.astype(BF16), preferred_element_type=F32)
        if first:
            acc_scr[...] = pv
        else:
            acc_scr[...] = jnp.exp2(m_scr[0:1, :] - m_new) * acc_scr[...] + pv
        m_scr[...] = bcast(m_new)

    put(s_meta, 1, META_PAD)
    put(scores(0), 0, tk)
    fold(1, META_PAD, vmeta_t, first=True)

    def step(c, parity, last):
        if not last:
            put(scores(c + 1), 1 - parity, tk)
        fold(parity, tk, vt_ref[0, 0, c])

    n_iter = (n_chunks - 1) // group

    def body(j, carry):
        for u in range(group):
            step(group * j + u, u % 2, False)
        return carry

    lax.fori_loop(0, n_iter, body, 0)
    for c in range(group * n_iter, n_chunks):
        step(c, c % 2, c == n_chunks - 1)
    return acc_scr[...]


def _normalised(acc, dv):
    return acc[0:dv, :] * (1.0 / acc[dv:dv + 1, :])


def _flash_cols(qt, k_ref, vt_ref, kmeta, vmeta_t, kmax, finish,
                s0_scr, s1_scr, c0_scr, c1_scr, m_scr, acc_scr):
    s_meta = _masked_meta_scores(kmeta, qt)
    acc_scr[...] = jnp.zeros_like(acc_scr)
    shift = _score_shift(qt, kmax)
    bounded = jnp.max(shift) <= SCORE_BOUND

    @pl.when(bounded)
    def _():
        finish(_flash_cols_shifted(qt, s_meta, shift, k_ref, vt_ref, vmeta_t, acc_scr))

    @pl.when(jnp.logical_not(bounded))
    def _():
        finish(_flash_cols_online(qt, s_meta, k_ref, vt_ref, vmeta_t,
                                  (s0_scr, s1_scr), (c0_scr, c1_scr), m_scr, acc_scr))


def _da_attn_kernel(qt_ref, k_ref, vt_ref, kmeta_ref, vmeta_ref, kmax_ref, lq1_ref, lk1_ref, lq2_ref, lk2_ref,
                    g_ref, o_ref, *scratch):
    tq = o_ref.shape[1]
    qt = jnp.concatenate([qt_ref[0, 0, 0, 0], qt_ref[0, 0, 0, 1]], axis=1)

    def finish(acc):
        at = _normalised(acc, DA_V_DIM)
        lam = (jnp.exp(jnp.sum(lq1_ref[...] * lk1_ref[...], axis=1, keepdims=True))
               - jnp.exp(jnp.sum(lq2_ref[...] * lk2_ref[...], axis=1, keepdims=True)) + LAMBDA_INIT)
        ot = at[:, :tq] - lam * at[:, tq:]
        inv = lax.rsqrt(jnp.mean(ot * ot, axis=0, keepdims=True) + RMS_EPS)
        o_ref[0] = ((ot * inv).T * g_ref[...] * (1.0 - LAMBDA_INIT)).astype(BF16)

    _flash_cols(qt, k_ref, vt_ref, kmeta_ref[0, 0], vmeta_ref[0, 0], kmax_ref[0, 0], finish, *scratch)


def _mla_attn_kernel(qt_ref, k_ref, vt_ref, kmeta_ref, vmeta_ref, kmax_ref, o_ref, *scratch):
    def finish(acc):
        o_ref[0] = _normalised(acc, MLA_V).T.astype(BF16)

    _flash_cols(qt_ref[0, 0, 0], k_ref, vt_ref, kmeta_ref[0, 0], vmeta_ref[0, 0], kmax_ref[0, 0], finish, *scratch)


def _attn_scratch(cols, tk):
    stat = pltpu.VMEM((SUBLANES, cols), F32)
    return [pltpu.VMEM((tk, cols), F32), pltpu.VMEM((tk, cols), F32), stat, stat, stat,
            pltpu.VMEM((V_ROWS, cols), F32)]


def _kmax_spec():
    return pl.BlockSpec((1, 1, 1, LANES), lambda b, h, i: (b, h, 0, 0))


def _attn_vmem(cols, tk, s, dk, dv):
    return (2 * s * (dk + dv) * 2 + 2 * dk * cols * 2 + 2 * tk * cols * 4 + dv * cols * 4
            + 4 * tk * cols * 4 + 2 * cols * dv * 2) / MIB + 8


def _da_attn(qdat, kda, vdat, kmeta, vmeta_t, kmax, lam_vecs, g):
    nb, nh, nq, _, _, tq = qdat.shape
    cols = 2 * tq
    s = kda.shape[2]
    nc, tk = vdat.shape[2], vdat.shape[4]
    vec_spec = pl.BlockSpec((1, DA_HEAD_DIM), lambda b, h, i: (0, 0))
    meta_spec = pl.BlockSpec((1, 1, META_PAD, LANES), lambda b, h, i: (0, h, 0, 0))
    vmeta_spec = pl.BlockSpec((1, 1, V_ROWS, META_PAD), lambda b, h, i: (0, h, 0, 0))
    return pl.pallas_call(
        _da_attn_kernel,
        grid=(nb, nh, nq),
        in_specs=[
            pl.BlockSpec((1, 1, 1, 2, LANES, tq), lambda b, h, i: (b, h, i, 0, 0, 0)),
            pl.BlockSpec((1, 1, s, LANES), lambda b, h, i: (b, h, 0, 0)),
            pl.BlockSpec((1, 1, nc, V_ROWS, tk), lambda b, h, i: (b, h, 0, 0, 0)),
            meta_spec, vmeta_spec, _kmax_spec(),
            vec_spec, vec_spec, vec_spec, vec_spec,
            pl.BlockSpec((1, DA_V_DIM), lambda b, h, i: (0, 0)),
        ],
        out_specs=pl.BlockSpec((1, tq, DA_V_DIM), lambda b, h, i: (b, i, h)),
        out_shape=jax.ShapeDtypeStruct((nb, s, DA_WIDTH), BF16),
        scratch_shapes=_attn_scratch(cols, tk),
        compiler_params=_cparams(3, _attn_vmem(cols, tk, s, LANES, DA_V_DIM)),
        name="da_attn",
    )(qdat, kda, vdat, kmeta, vmeta_t, kmax, *lam_vecs, g)


def _mla_attn(qmt, km, vmt, kmeta, vmeta_t, kmax):
    nb, nh, nq, _, tq = qmt.shape
    s = km.shape[2]
    nc, tk = vmt.shape[2], vmt.shape[4]
    return pl.pallas_call(
        _mla_attn_kernel,
        grid=(nb, nh, nq),
        in_specs=[
            pl.BlockSpec((1, 1, 1, MLA_QK_PAD, tq), lambda b, h, i: (b, h, i, 0, 0)),
            pl.BlockSpec((1, 1, s, MLA_QK_PAD), lambda b, h, i: (b, h, 0, 0)),
            pl.BlockSpec((1, 1, nc, V_ROWS, tk), lambda b, h, i: (b, h, 0, 0, 0)),
            pl.BlockSpec((1, 1, META_PAD, MLA_QK_PAD), lambda b, h, i: (0, h, 0, 0)),
            pl.BlockSpec((1, 1, V_ROWS, META_PAD), lambda b, h, i: (0, h, 0, 0)),
            _kmax_spec(),
        ],
        out_specs=pl.BlockSpec((1, tq, MLA_V), lambda b, h, i: (b, i, h)),
        out_shape=jax.ShapeDtypeStruct((nb, s, MLA_WIDTH), BF16),
        scratch_shapes=_attn_scratch(tq, tk),
        compiler_params=_cparams(3, _attn_vmem(tq, tk, s, MLA_QK_PAD, MLA_V)),
        name="mla_attn",
    )(qmt, km, vmt, kmeta, vmeta_t, kmax)


def _outproj_ln_kernel(h_ref, oa_ref, om_ref, wo_ref, g_ref, b_ref, o_ref):
    tm = h_ref.shape[0]
    sub = tm // OUTPROJ_SUBBLOCKS if tm % (OUTPROJ_SUBBLOCKS * BF16_SUBLANES) == 0 else tm
    for r0 in range(0, tm, sub):
        rows = pl.ds(r0, sub)
        mix = (jnp.dot(oa_ref[rows, :], wo_ref[0:DA_WIDTH, :], preferred_element_type=F32)
               + jnp.dot(om_ref[rows, :], wo_ref[DA_WIDTH:DA_WIDTH + MLA_WIDTH, :], preferred_element_type=F32))
        o_ref[rows, :] = _layer_norm(ALPHA * h_ref[rows, :] + mix, g_ref[...], b_ref[...])


def _outproj_ln(h, oa, om, wo, g, b, *, tm=512):
    m, d = h.shape
    tm = min(tm, m)
    assert m % tm == 0
    vmem = (2 * 2 * tm * d * 4 + 2 * 2 * tm * DA_WIDTH * 2 + wo.size * 2 + 3 * tm * d * 4) / MIB + 6
    return pl.pallas_call(
        _outproj_ln_kernel,
        grid=(m // tm,),
        in_specs=[
            pl.BlockSpec((tm, d), lambda i: (i, 0)),
            pl.BlockSpec((tm, DA_WIDTH), lambda i: (i, 0)),
            pl.BlockSpec((tm, MLA_WIDTH), lambda i: (i, 0)),
            _resident(wo.shape), _resident(g.shape), _resident(b.shape),
        ],
        out_specs=pl.BlockSpec((tm, d), lambda i: (i, 0)),
        out_shape=jax.ShapeDtypeStruct((m, d), F32),
        compiler_params=_cparams(1, vmem),
        name="outproj_ln",
    )(h, oa, om, wo, g, b)


def _rope_lane_tables(n_pos, rot_dim, period):
    half = rot_dim // 2
    inv = 1.0 / (ROPE_THETA ** (jnp.arange(0, rot_dim, 2, dtype=F32) / rot_dim))
    ang = jnp.arange(n_pos, dtype=F32)[:, None] * inv[None, :]
    cos, sin = jnp.cos(ang), jnp.sin(ang)
    within = jnp.arange(LANES) % period
    idx = within % half
    c = jnp.where(within < rot_dim, cos[:, idx], 1.0)
    sp = jnp.where((within >= half) & (within < rot_dim), sin[:, idx], 0.0)
    sm = jnp.where(within < half, -sin[:, idx], 0.0)
    return c, sp, sm


def _prep_weights(p):
    w = {}
    for name in ("ffn1_w_gate", "ffn1_w_up", "ffn2_w_gate", "ffn2_w_up", "w_o", "w_ukv"):
        w[name] = p[name][0].astype(BF16)
    for name in ("ffn1_w_down", "ffn2_w_down"):
        w[name] = (p[name][0] * 0.5).astype(BF16)
    w_in = p["w_in"][0]
    w["w_in"] = jnp.pad(w_in, ((0, 0), (0, LANES - MLA_ROPE))).astype(BF16)
    w_uq = p["w_uq"][0].reshape(MLA_Q_RANK, MLA_HEADS, MLA_NOPE + MLA_ROPE)
    w_uq = jnp.pad(w_uq, ((0, 0), (0, 0), (0, MLA_QK_PAD - MLA_NOPE - MLA_ROPE)))
    w["w_uq"] = w_uq.reshape(MLA_Q_RANK, MLA_HEADS * MLA_QK_PAD).astype(BF16)
    for name in ("ln1_g", "ln1_b", "ln2_g", "ln2_b", "ln3_g", "ln3_b", "da_subln_g", "mla_q_norm_g", "mla_kv_norm_g",
                 "da_lambda_q1", "da_lambda_k1", "da_lambda_q2", "da_lambda_k2"):
        w[name] = p[name].astype(F32)
    return w


def _mixer_inputs(h, nb, s, w, tabs_da, tabs_mla):
    return _inproj(h, nb, s, w["w_in"], w["mla_q_norm_g"], w["w_uq"], w["mla_kv_norm_g"], w["w_ukv"],
                   tabs_da, tabs_mla)


def _trunk(x, meta_kv, w, tabs_da, tabs_mla):
    nb, s, d = x.shape
    h1 = _ffn_ln(x.reshape(nb * s, d), w["ffn1_w_gate"], w["ffn1_w_up"], w["ffn1_w_down"], w["ln1_g"], w["ln1_b"])
    qdat, kda, vdat, qmt, km, vmt, knorm2 = _mixer_inputs(h1, nb, s, w, tabs_da, tabs_mla)
    kda_m, vdat_m, km_m, vmt_m, knorm2_m = meta_kv
    knorm2 = jnp.max(knorm2.reshape(nb, -1, DA_HEADS + MLA_HEADS, LANES), axis=1)
    kmax = jnp.sqrt(jnp.maximum(knorm2, knorm2_m))[:, :, None, :]
    lam_vecs = (w["da_lambda_q1"], w["da_lambda_k1"], w["da_lambda_q2"], w["da_lambda_k2"])
    oa = _da_attn(qdat, kda, vdat, kda_m, vdat_m, kmax[:, :DA_HEADS], lam_vecs, w["da_subln_g"])
    om = _mla_attn(qmt, km, vmt, km_m, vmt_m, kmax[:, DA_HEADS:])
    h2 = _outproj_ln(h1, oa.reshape(nb * s, DA_WIDTH), om.reshape(nb * s, MLA_WIDTH), w["w_o"], w["ln2_g"], w["ln2_b"])
    y = _ffn_ln(h2, w["ffn2_w_gate"], w["ffn2_w_up"], w["ffn2_w_down"], w["ln3_g"], w["ln3_b"])
    return y.reshape(nb, s, d)


def kernel(x_prompt, x_sample, meta_tokens, ffn1_w_gate, ffn1_w_up, ffn1_w_down, ln1_g, ln1_b, w_in, da_lambda_q1, da_lambda_k1, da_lambda_q2, da_lambda_k2, da_subln_g, mla_q_norm_g, w_uq, mla_kv_norm_g, w_ukv, w_o, ln2_g, ln2_b, ffn2_w_gate, ffn2_w_up, ffn2_w_down, ln3_g, ln3_b):
    p = dict(ffn1_w_gate=ffn1_w_gate, ffn1_w_up=ffn1_w_up, ffn1_w_down=ffn1_w_down, ln1_g=ln1_g, ln1_b=ln1_b,
             w_in=w_in, da_lambda_q1=da_lambda_q1, da_lambda_k1=da_lambda_k1, da_lambda_q2=da_lambda_q2,
             da_lambda_k2=da_lambda_k2, da_subln_g=da_subln_g, mla_q_norm_g=mla_q_norm_g, w_uq=w_uq,
             mla_kv_norm_g=mla_kv_norm_g, w_ukv=w_ukv, w_o=w_o, ln2_g=ln2_g, ln2_b=ln2_b,
             ffn2_w_gate=ffn2_w_gate, ffn2_w_up=ffn2_w_up, ffn2_w_down=ffn2_w_down, ln3_g=ln3_g, ln3_b=ln3_b)
    w = _prep_weights(p)
    s_max = max(x_prompt.shape[1], x_sample.shape[1])
    tabs_da = _rope_lane_tables(N_META + s_max, DA_ROT, DA_HEAD_DIM)
    tabs_mla = _rope_lane_tables(N_META + s_max, MLA_ROPE, LANES)

    hm = _ffn_ln(meta_tokens.astype(F32), w["ffn1_w_gate"], w["ffn1_w_up"], w["ffn1_w_down"], w["ln1_g"], w["ln1_b"])
    hm = jnp.pad(hm, ((0, PROJ_ROWS - N_META), (0, 0)))
    _, kda_m, vdat_m, _, km_m, vmt_m, knorm2_m = _mixer_inputs(
        hm, 1, PROJ_ROWS, w, [t[:PROJ_ROWS] for t in tabs_da], [t[:PROJ_ROWS] for t in tabs_mla])
    meta_kv = (kda_m[:, :, :META_PAD], vdat_m[:, :, 0, :, :META_PAD], km_m[:, :, :META_PAD],
               vmt_m[:, :, 0, :, :META_PAD], knorm2_m)

    seq_da = [t[N_META:] for t in tabs_da]
    seq_mla = [t[N_META:] for t in tabs_mla]
    y_prompt = _trunk(x_prompt, meta_kv, w, seq_da, seq_mla)
    y_sample = _trunk(x_sample, meta_kv, w, seq_da, seq_mla)
    return (y_prompt, y_sample)
```

```python
import functools
import math

import jax
import jax.numpy as jnp
from jax import lax
from jax.experimental import pallas as pl
from jax.experimental.pallas import tpu as pltpu

N_META = 16
ROPE_THETA = 500000.0
DA_HEADS = 8
DA_HEAD_DIM = 64
DA_V_DIM = 2 * DA_HEAD_DIM
DA_ROT = DA_HEAD_DIM // 4
DA_QK_COLS = DA_HEADS * 2 * DA_HEAD_DIM
DA_WIDTH = DA_HEADS * DA_V_DIM
MLA_HEADS = 8
MLA_Q_RANK = 512
MLA_KV_RANK = 256
MLA_NOPE = 128
MLA_ROPE = 64
MLA_V = 128
MLA_WIDTH = MLA_HEADS * MLA_V
MLA_QK_PAD = 256
DEPTH = 1
ALPHA = (2 * DEPTH) ** 0.25
LN_EPS = 1e-5
RMS_EPS = 1e-6
LAMBDA_INIT = 0.8 - 0.6 * math.exp(-0.3 * 0)
LOG2E = 1.4426950408889634

LANES = 128
SUBLANES = 8
BF16_SUBLANES = 16
META_PAD = 128
NEG_BIG = -1e30

PROJ_ROWS = 256
DA_Q_TILE = 1024
MLA_Q_TILE = 2048
KV_CHUNK = 512
CHUNKS_PER_ITER = 8
ONLINE_CHUNKS_PER_ITER = 2
OUTPROJ_SUBBLOCKS = 2
SCORE_BOUND = 45.0
BOUND_SLACK = 1.01
V_EXTRA_ROWS = BF16_SUBLANES
V_ROWS = DA_V_DIM + V_EXTRA_ROWS

F32 = jnp.float32
BF16 = jnp.bfloat16
MIB = 1024 * 1024


def _cparams(n_axes, vmem_mib):
    return pltpu.CompilerParams(
        dimension_semantics=("arbitrary",) * n_axes,
        vmem_limit_bytes=int(vmem_mib * MIB),
    )


def _resident(shape):
    n = len(shape)
    return pl.BlockSpec(shape, lambda *_: (0,) * n, pipeline_mode=pl.Buffered(1))


def _layer_norm(y, g, b):
    mu = jnp.mean(y, axis=-1, keepdims=True)
    d = y - mu
    var = jnp.mean(d * d, axis=-1, keepdims=True)
    return d * lax.rsqrt(var + LN_EPS) * g + b


def _rms_norm(x, g):
    return x * lax.rsqrt(jnp.mean(x * x, axis=-1, keepdims=True) + RMS_EPS) * g


def _ffn_ln_kernel(x_ref, wg_ref, wu_ref, wd_ref, g_ref, b_ref, o_ref, xb_ref, acc_ref, y_ref, *, n_slices):
    i, j = pl.program_id(0), pl.program_id(1)
    n_blocks, n_chunks = pl.num_programs(0) - 1, pl.num_programs(1)
    slice_rows = y_ref.shape[0] // n_slices

    def norm_slice():
        r0 = pl.multiple_of(jnp.minimum(j, n_slices - 1) * slice_rows, slice_rows)
        rows = pl.ds(r0, slice_rows)
        o_ref[rows, :] = _layer_norm(y_ref[rows, :], g_ref[...], b_ref[...])

    @pl.when(i < n_blocks)
    def _():
        @pl.when(j == 0)
        def _():
            xb_ref[...] = x_ref[...].astype(BF16)
            acc_ref[...] = jnp.zeros_like(acc_ref)

        @pl.when((i == 0) & (j == 0))
        def _():
            y_ref[...] = jnp.zeros_like(y_ref)

        norm_slice()
        xb = xb_ref[...]
        gate = jnp.dot(xb, wg_ref[...], preferred_element_type=F32)
        up = jnp.dot(xb, wu_ref[...], preferred_element_type=F32)
        hid = (gate * (1.0 / (1.0 + jnp.exp(-gate)))) * up
        acc_ref[...] += jnp.dot(hid.astype(BF16), wd_ref[...], preferred_element_type=F32)

        @pl.when(j == n_chunks - 1)
        def _():
            y_ref[...] = ALPHA * x_ref[...] + acc_ref[...]

    @pl.when(i == n_blocks)
    def _():
        norm_slice()


def _ffn_ln(x, wg, wu, wd, g, b, *, tm=512, tf=512):
    m, d = x.shape
    f = wg.shape[1]
    tm = min(tm, m)
    tf = min(tf, f)
    assert m % tm == 0 and f % tf == 0
    nb, nj = m // tm, f // tf
    n_slices = 1
    while 2 * n_slices <= min(nj, tm // SUBLANES):
        n_slices *= 2
    assert tm % (n_slices * SUBLANES) == 0

    def chunk(i, j):
        return jnp.where(i < nb, j, nj - 1)

    vmem = (2 * 2 * tm * d * 4 + 2 * 3 * d * tf * 2 + tm * d * 10 + 4 * tm * tf * 4) / MIB + 6
    return pl.pallas_call(
        functools.partial(_ffn_ln_kernel, n_slices=n_slices),
        grid=(nb + 1, nj),
        in_specs=[
            pl.BlockSpec((tm, d), lambda i, j: (jnp.minimum(i, nb - 1), 0)),
            pl.BlockSpec((d, tf), lambda i, j: (0, chunk(i, j))),
            pl.BlockSpec((d, tf), lambda i, j: (0, chunk(i, j))),
            pl.BlockSpec((tf, d), lambda i, j: (chunk(i, j), 0)),
            pl.BlockSpec((1, d), lambda i, j: (0, 0)),
            pl.BlockSpec((1, d), lambda i, j: (0, 0)),
        ],
        out_specs=pl.BlockSpec((tm, d), lambda i, j: (jnp.maximum(i - 1, 0), 0)),
        out_shape=jax.ShapeDtypeStruct((m, d), F32),
        scratch_shapes=[pltpu.VMEM((tm, d), BF16), pltpu.VMEM((tm, d), F32), pltpu.VMEM((tm, d), F32)],
        compiler_params=_cparams(2, vmem),
        name="ffn_ln",
    )(x, wg, wu, wd, g, b)


def _rope(x, c, sp, sm, half):
    return x * c + pltpu.roll(x, half, 1) * sp + pltpu.roll(x, LANES - half, 1) * sm


def _inproj_kernel(h_ref, win_ref, gq_ref, wuq_ref, gkv_ref, wukv_ref,
                   cda_ref, spda_ref, smda_ref, cm_ref, spm_ref, smm_ref,
                   qdat_ref, kda_ref, vdat_ref, qmt_ref, km_ref, vmt_ref, knorm_ref):
    tm = h_ref.shape[0]
    hb = h_ref[...].astype(BF16)

    def proj(lo, hi):
        return jnp.dot(hb, win_ref[:, lo:hi], preferred_element_type=F32)

    def row_sq_norm(x):
        return jnp.sum(x * x, axis=1, keepdims=True)

    def put_max_sq_norm(row, n2):
        knorm_ref[0, row:row + 1, :] = jnp.broadcast_to(jnp.max(n2, axis=0, keepdims=True), (1, LANES))

    cda, spda, smda = cda_ref[...], spda_ref[...], smda_ref[...]
    cm, spm, smm = cm_ref[...], spm_ref[...], smm_ref[...]
    da_half = DA_ROT // 2
    mla_half = MLA_ROPE // 2
    map1_dims = lax.broadcasted_iota(jnp.int32, (LANES, tm), 0) < DA_HEAD_DIM
    da_scale = DA_HEAD_DIM ** -0.5 * LOG2E
    mla_scale = (MLA_NOPE + MLA_ROPE) ** -0.5 * LOG2E
    sum_rows = (lax.broadcasted_iota(jnp.int32, (V_EXTRA_ROWS, tm), 0) == 0).astype(BF16)

    c0 = 2 * DA_QK_COLS + DA_WIDTH
    cq = _rms_norm(proj(c0, c0 + MLA_Q_RANK), gq_ref[...]).astype(BF16)
    c0 += MLA_Q_RANK
    qm = jnp.dot(cq, wuq_ref[...], preferred_element_type=F32) * mla_scale
    for h in range(MLA_HEADS):
        base = h * MLA_QK_PAD
        qmt_ref[0, h, 0, 0:MLA_NOPE, :] = qm[:, base:base + MLA_NOPE].T.astype(BF16)
        qmt_ref[0, h, 0, MLA_NOPE:MLA_QK_PAD, :] = _rope(
            qm[:, base + MLA_NOPE:base + MLA_QK_PAD], cm, spm, smm, mla_half).T.astype(BF16)

    ckv = _rms_norm(proj(c0, c0 + MLA_KV_RANK), gkv_ref[...]).astype(BF16)
    c0 += MLA_KV_RANK
    kv = jnp.dot(ckv, wukv_ref[...], preferred_element_type=F32)
    kr_f = _rope(proj(c0, c0 + LANES), cm, spm, smm, mla_half)
    kr = kr_f.astype(BF16)
    kr_n2 = row_sq_norm(kr_f)
    for h in range(MLA_HEADS):
        base = h * (MLA_NOPE + MLA_V)
        k_nope = kv[:, base:base + MLA_NOPE]
        km_ref[0, h, :, 0:MLA_NOPE] = k_nope.astype(BF16)
        km_ref[0, h, :, MLA_NOPE:MLA_QK_PAD] = kr
        put_max_sq_norm(DA_HEADS + h, row_sq_norm(k_nope) + kr_n2)
        vmt_ref[0, h, 0, 0:MLA_V, :] = kv[:, base + MLA_NOPE:base + MLA_NOPE + MLA_V].T.astype(BF16)
        vmt_ref[0, h, 0, MLA_V:V_ROWS, :] = sum_rows

    c0 = 0
    zq = proj(c0, c0 + DA_QK_COLS)
    for h in range(DA_HEADS):
        rt = (_rope(zq[:, h * LANES:(h + 1) * LANES], cda, spda, smda, da_half) * da_scale).T
        qdat_ref[0, h, 0, 0] = jnp.where(map1_dims, rt, 0.0).astype(BF16)
        qdat_ref[0, h, 0, 1] = jnp.where(map1_dims, 0.0, rt).astype(BF16)
    c0 += DA_QK_COLS
    zk = proj(c0, c0 + DA_QK_COLS)
    for h in range(DA_HEADS):
        k = _rope(zk[:, h * LANES:(h + 1) * LANES], cda, spda, smda, da_half)
        kda_ref[0, h] = k.astype(BF16)
        put_max_sq_norm(h, row_sq_norm(k))
    c0 += DA_QK_COLS
    zv = proj(c0, c0 + DA_WIDTH)
    for h in range(DA_HEADS):
        vdat_ref[0, h, 0, 0:DA_V_DIM, :] = zv[:, h * DA_V_DIM:(h + 1) * DA_V_DIM].T.astype(BF16)
        vdat_ref[0, h, 0, DA_V_DIM:V_ROWS, :] = sum_rows


def _inproj(h, nb, s, win, gq, wuq, gkv, wukv, tabs_da, tabs_mla):
    m, d = h.shape
    tm = PROJ_ROWS
    ck = min(KV_CHUNK, s)
    tqd = min(DA_Q_TILE, s)
    tqm = min(MLA_Q_TILE, s)
    assert m == nb * s and all(t % tm == 0 and s % t == 0 for t in (tm, ck, tqd, tqm))
    spb = s // tm
    tab_spec = pl.BlockSpec((tm, LANES), lambda i: (i % spb, 0))

    def rows_spec(width):
        return pl.BlockSpec((1, DA_HEADS, tm, width), lambda i: (i // spb, 0, i % spb, 0))

    def cols_spec(rows, tile, *mid):
        per = tile // tm
        zeros = (0,) * (len(mid) + 1)
        return pl.BlockSpec((1, DA_HEADS, 1) + mid + (rows, tm),
                            lambda i: (i // spb, 0, (i % spb) // per) + zeros + ((i % spb) % per,))

    def shape(*tail):
        return jax.ShapeDtypeStruct((nb, DA_HEADS) + tail, BF16)

    vmem = (2 * tm * d * 4 + (win.size + wuq.size + wukv.size) * 2 + 2 * 6 * tm * LANES * 4
            + 2 * tm * 8 * (256 + 128 + 128 + 256 + 256 + 128) * 2 + 8 * tm * 2048 * 4) / MIB + 6
    return pl.pallas_call(
        _inproj_kernel,
        grid=(m // tm,),
        in_specs=[
            pl.BlockSpec((tm, d), lambda i: (i, 0)),
            _resident(win.shape), _resident(gq.shape), _resident(wuq.shape),
            _resident(gkv.shape), _resident(wukv.shape),
            tab_spec, tab_spec, tab_spec, tab_spec, tab_spec, tab_spec,
        ],
        out_specs=[
            cols_spec(LANES, tqd, 2),
            rows_spec(LANES), cols_spec(V_ROWS, ck),
            cols_spec(MLA_QK_PAD, tqm), rows_spec(MLA_QK_PAD), cols_spec(V_ROWS, ck),
            pl.BlockSpec((1, DA_HEADS + MLA_HEADS, LANES), lambda i: (i, 0, 0)),
        ],
        out_shape=[
            shape(s // tqd, 2, LANES, tqd), shape(s, LANES), shape(s // ck, V_ROWS, ck),
            shape(s // tqm, MLA_QK_PAD, tqm), shape(s, MLA_QK_PAD), shape(s // ck, V_ROWS, ck),
            jax.ShapeDtypeStruct((m // tm, DA_HEADS + MLA_HEADS, LANES), F32),
        ],
        compiler_params=_cparams(1, vmem),
        name="inproj",
    )(h, win, gq, wuq, gkv, wukv, *tabs_da, *tabs_mla)


def _masked_meta_scores(kmeta, qt):
    s = jnp.dot(kmeta, qt, preferred_element_type=F32)
    return jnp.where(lax.broadcasted_iota(jnp.int32, s.shape, 0) < N_META, s, NEG_BIG)


def _score_shift(qt, kmax):
    qf = qt.astype(F32)
    return jnp.sqrt(jnp.sum(qf * qf, axis=0, keepdims=True)) * (kmax[0:1, 0:1] * BOUND_SLACK)


def _flash_cols_shifted(qt, s_meta, shift, k_ref, vt_ref, vmeta_t, acc_scr):
    n_chunks, tk = vt_ref.shape[2], vt_ref.shape[4]
    group = CHUNKS_PER_ITER

    def weights(s):
        return jnp.exp2(s - shift).astype(BF16)

    def chunk_pv(c):
        off = c * tk if isinstance(c, int) else pl.multiple_of(c * tk, tk)
        s = jnp.dot(k_ref[0, 0, pl.ds(off, tk), :], qt, preferred_element_type=F32)
        return jnp.dot(vt_ref[0, 0, c], weights(s), preferred_element_type=F32)

    acc_scr[...] = jnp.zeros_like(acc_scr)

    def body(j, carry):
        total = chunk_pv(group * j)
        for u in range(1, group):
            total = total + chunk_pv(group * j + u)
        acc_scr[...] += total
        return carry

    n_iter = n_chunks // group if n_chunks >= 2 * group else 0
    lax.fori_loop(0, n_iter, body, 0)
    for c in range(group * n_iter, n_chunks):
        acc_scr[...] += chunk_pv(c)
    return acc_scr[...] + jnp.dot(vmeta_t, weights(s_meta), preferred_element_type=F32)


def _flash_cols_online(qt, s_meta, k_ref, vt_ref, vmeta_t, s_scrs, cmax_scrs, m_scr, acc_scr):
    n_chunks, tk = vt_ref.shape[2], vt_ref.shape[4]
    r = qt.shape[1]
    group = ONLINE_CHUNKS_PER_ITER
    assert group % 2 == 0

    def bcast(x):
        return jnp.broadcast_to(x, (SUBLANES, r))

    def scores(c):
        off = c * tk if isinstance(c, int) else pl.multiple_of(c * tk, tk)
        return jnp.dot(k_ref[0, 0, pl.ds(off, tk), :], qt, preferred_element_type=F32)

    def put(s, slot, rows):
        s_scrs[slot][0:rows, :] = s
        cmax_scrs[slot][...] = bcast(jnp.max(s, axis=0, keepdims=True))

    def fold(slot, rows, vt, first=False):
        cmax = cmax_scrs[slot][0:1, :]
        m_new = cmax if first else jnp.maximum(m_scr[0:1, :], cmax)
        p = jnp.exp2(s_scrs[slot][0:rows, :] - m_new)
        pv = jnp.dot(vt, p.astype(BF16), preferred_element_type=F32)
        if first:
            acc_scr[...] = pv
        else:
            acc_scr[...] = jnp.exp2(m_scr[0:1, :] - m_new) * acc_scr[...] + pv
        m_scr[...] = bcast(m_new)

    put(s_meta, 1, META_PAD)
    put(scores(0), 0, tk)
    fold(1, META_PAD, vmeta_t, first=True)

    def step(c, parity, last):
        if not last:
            put(scores(c + 1), 1 - parity, tk)
        fold(parity, tk, vt_ref[0, 0, c])

    n_iter = (n_chunks - 1) // group

    def body(j, carry):
        for u in range(group):
            step(group * j + u, u % 2, False)
        return carry

    lax.fori_loop(0, n_iter, body, 0)
    for c in range(group * n_iter, n_chunks):
        step(c, c % 2, c == n_chunks - 1)
    return acc_scr[...]


def _normalised(acc, dv):
    return acc[0:dv, :] * (1.0 / acc[dv:dv + 1, :])


def _flash_cols(qt, k_ref, vt_ref, kmeta, vmeta_t, kmax, finish,
                s0_scr, s1_scr, c0_scr, c1_scr, m_scr, acc_scr):
    s_meta = _masked_meta_scores(kmeta, qt)
    shift = _score_shift(qt, kmax)
    bounded = jnp.max(shift) <= SCORE_BOUND

    @pl.when(bounded)
    def _():
        finish(_flash_cols_shifted(qt, s_meta, shift, k_ref, vt_ref, vmeta_t, acc_scr))

    @pl.when(jnp.logical_not(bounded))
    def _():
        finish(_flash_cols_online(qt, s_meta, k_ref, vt_ref, vmeta_t,
                                  (s0_scr, s1_scr), (c0_scr, c1_scr), m_scr, acc_scr))


def _da_attn_kernel(qt_ref, k_ref, vt_ref, kmeta_ref, vmeta_ref, kmax_ref, lq1_ref, lk1_ref, lq2_ref, lk2_ref,
                    g_ref, o_ref, *scratch):
    tq = o_ref.shape[1]
    qt = jnp.concatenate([qt_ref[0, 0, 0, 0], qt_ref[0, 0, 0, 1]], axis=1)

    def finish(acc):
        at = _normalised(acc, DA_V_DIM)
        lam = (jnp.exp(jnp.sum(lq1_ref[...] * lk1_ref[...], axis=1, keepdims=True))
               - jnp.exp(jnp.sum(lq2_ref[...] * lk2_ref[...], axis=1, keepdims=True)) + LAMBDA_INIT)
        ot = at[:, :tq] - lam * at[:, tq:]
        inv = lax.rsqrt(jnp.mean(ot * ot, axis=0, keepdims=True) + RMS_EPS)
        o_ref[0] = ((ot * inv).T * g_ref[...] * (1.0 - LAMBDA_INIT)).astype(BF16)

    _flash_cols(qt, k_ref, vt_ref, kmeta_ref[0, 0], vmeta_ref[0, 0], kmax_ref[0, 0], finish, *scratch)


def _mla_attn_kernel(qt_ref, k_ref, vt_ref, kmeta_ref, vmeta_ref, kmax_ref, o_ref, *scratch):
    def finish(acc):
        o_ref[0] = _normalised(acc, MLA_V).T.astype(BF16)

    _flash_cols(qt_ref[0, 0, 0], k_ref, vt_ref, kmeta_ref[0, 0], vmeta_ref[0, 0], kmax_ref[0, 0], finish, *scratch)


def _attn_scratch(cols, tk):
    stat = pltpu.VMEM((SUBLANES, cols), F32)
    return [pltpu.VMEM((tk, cols), F32), pltpu.VMEM((tk, cols), F32), stat, stat, stat,
            pltpu.VMEM((V_ROWS, cols), F32)]


def _kmax_spec():
    return pl.BlockSpec((1, 1, 1, LANES), lambda b, h, i: (b, h, 0, 0))


def _attn_vmem(cols, tk, s, dk, dv):
    return (2 * s * (dk + dv) * 2 + 2 * dk * cols * 2 + 2 * tk * cols * 4 + dv * cols * 4
            + 4 * tk * cols * 4 + 2 * cols * dv * 2) / MIB + 8


def _da_attn(qdat, kda, vdat, kmeta, vmeta_t, kmax, lam_vecs, g):
    nb, nh, nq, _, _, tq = qdat.shape
    cols = 2 * tq
    s = kda.shape[2]
    nc, tk = vdat.shape[2], vdat.shape[4]
    vec_spec = pl.BlockSpec((1, DA_HEAD_DIM), lambda b, h, i: (0, 0))
    meta_spec = pl.BlockSpec((1, 1, META_PAD, LANES), lambda b, h, i: (0, h, 0, 0))
    vmeta_spec = pl.BlockSpec((1, 1, V_ROWS, META_PAD), lambda b, h, i: (0, h, 0, 0))
    return pl.pallas_call(
        _da_attn_kernel,
        grid=(nb, nh, nq),
        in_specs=[
            pl.BlockSpec((1, 1, 1, 2, LANES, tq), lambda b, h, i: (b, h, i, 0, 0, 0)),
            pl.BlockSpec((1, 1, s, LANES), lambda b, h, i: (b, h, 0, 0)),
            pl.BlockSpec((1, 1, nc, V_ROWS, tk), lambda b, h, i: (b, h, 0, 0, 0)),
            meta_spec, vmeta_spec, _kmax_spec(),
            vec_spec, vec_spec, vec_spec, vec_spec,
            pl.BlockSpec((1, DA_V_DIM), lambda b, h, i: (0, 0)),
        ],
        out_specs=pl.BlockSpec((1, tq, DA_V_DIM), lambda b, h, i: (b, i, h)),
        out_shape=jax.ShapeDtypeStruct((nb, s, DA_WIDTH), BF16),
        scratch_shapes=_attn_scratch(cols, tk),
        compiler_params=_cparams(3, _attn_vmem(cols, tk, s, LANES, DA_V_DIM)),
        name="da_attn",
    )(qdat, kda, vdat, kmeta, vmeta_t, kmax, *lam_vecs, g)


def _mla_attn(qmt, km, vmt, kmeta, vmeta_t, kmax):
    nb, nh, nq, _, tq = qmt.shape
    s = km.shape[2]
    nc, tk = vmt.shape[2], vmt.shape[4]
    return pl.pallas_call(
        _mla_attn_kernel,
        grid=(nb, nh, nq),
        in_specs=[
            pl.BlockSpec((1, 1, 1, MLA_QK_PAD, tq), lambda b, h, i: (b, h, i, 0, 0)),
            pl.BlockSpec((1, 1, s, MLA_QK_PAD), lambda b, h, i: (b, h, 0, 0)),
            pl.BlockSpec((1, 1, nc, V_ROWS, tk), lambda b, h, i: (b, h, 0, 0, 0)),
            pl.BlockSpec((1, 1, META_PAD, MLA_QK_PAD), lambda b, h, i: (0, h, 0, 0)),
            pl.BlockSpec((1, 1, V_ROWS, META_PAD), lambda b, h, i: (0, h, 0, 0)),
            _kmax_spec(),
        ],
        out_specs=pl.BlockSpec((1, tq, MLA_V), lambda b, h, i: (b, i, h)),
        out_shape=jax.ShapeDtypeStruct((nb, s, MLA_WIDTH), BF16),
        scratch_shapes=_attn_scratch(tq, tk),
        compiler_params=_cparams(3, _attn_vmem(tq, tk, s, MLA_QK_PAD, MLA_V)),
        name="mla_attn",
    )(qmt, km, vmt, kmeta, vmeta_t, kmax)


def _outproj_ln_kernel(h_ref, oa_ref, om_ref, wo_ref, g_ref, b_ref, o_ref):
    tm = h_ref.shape[0]
    sub = tm // OUTPROJ_SUBBLOCKS if tm % (OUTPROJ_SUBBLOCKS * BF16_SUBLANES) == 0 else tm
    for r0 in range(0, tm, sub):
        rows = pl.ds(r0, sub)
        mix = (jnp.dot(oa_ref[rows, :], wo_ref[0:DA_WIDTH, :], preferred_element_type=F32)
               + jnp.dot(om_ref[rows, :], wo_ref[DA_WIDTH:DA_WIDTH + MLA_WIDTH, :], preferred_element_type=F32))
        o_ref[rows, :] = _layer_norm(ALPHA * h_ref[rows, :] + mix, g_ref[...], b_ref[...])


def _outproj_ln(h, oa, om, wo, g, b, *, tm=512):
    m, d = h.shape
    tm = min(tm, m)
    assert m % tm == 0
    vmem = (2 * 2 * tm * d * 4 + 2 * 2 * tm * DA_WIDTH * 2 + wo.size * 2 + 3 * tm * d * 4) / MIB + 6
    return pl.pallas_call(
        _outproj_ln_kernel,
        grid=(m // tm,),
        in_specs=[
            pl.BlockSpec((tm, d), lambda i: (i, 0)),
            pl.BlockSpec((tm, DA_WIDTH), lambda i: (i, 0)),
            pl.BlockSpec((tm, MLA_WIDTH), lambda i: (i, 0)),
            _resident(wo.shape), _resident(g.shape), _resident(b.shape),
        ],
        out_specs=pl.BlockSpec((tm, d), lambda i: (i, 0)),
        out_shape=jax.ShapeDtypeStruct((m, d), F32),
        compiler_params=_cparams(1, vmem),
        name="outproj_ln",
    )(h, oa, om, wo, g, b)


def _rope_lane_tables(n_pos, rot_dim, period):
    half = rot_dim // 2
    inv = 1.0 / (ROPE_THETA ** (jnp.arange(0, rot_dim, 2, dtype=F32) / rot_dim))
    ang = jnp.arange(n_pos, dtype=F32)[:, None] * inv[None, :]
    cos, sin = jnp.cos(ang), jnp.sin(ang)
    within = jnp.arange(LANES) % period
    idx = within % half
    c = jnp.where(within < rot_dim, cos[:, idx], 1.0)
    sp = jnp.where((within >= half) & (within < rot_dim), sin[:, idx], 0.0)
    sm = jnp.where(within < half, -sin[:, idx], 0.0)
    return c, sp, sm


def _prep_weights(p):
    w = {}
    for name in ("ffn1_w_gate", "ffn1_w_up", "ffn2_w_gate", "ffn2_w_up", "w_o", "w_ukv"):
        w[name] = p[name][0].astype(BF16)
    for name in ("ffn1_w_down", "ffn2_w_down"):
        w[name] = (p[name][0] * 0.5).astype(BF16)
    w_in = p["w_in"][0]
    w["w_in"] = jnp.pad(w_in, ((0, 0), (0, LANES - MLA_ROPE))).astype(BF16)
    w_uq = p["w_uq"][0].reshape(MLA_Q_RANK, MLA_HEADS, MLA_NOPE + MLA_ROPE)
    w_uq = jnp.pad(w_uq, ((0, 0), (0, 0), (0, MLA_QK_PAD - MLA_NOPE - MLA_ROPE)))
    w["w_uq"] = w_uq.reshape(MLA_Q_RANK, MLA_HEADS * MLA_QK_PAD).astype(BF16)
    for name in ("ln1_g", "ln1_b", "ln2_g", "ln2_b", "ln3_g", "ln3_b", "da_subln_g", "mla_q_norm_g", "mla_kv_norm_g",
                 "da_lambda_q1", "da_lambda_k1", "da_lambda_q2", "da_lambda_k2"):
        w[name] = p[name].astype(F32)
    return w


def _mixer_inputs(h, nb, s, w, tabs_da, tabs_mla):
    return _inproj(h, nb, s, w["w_in"], w["mla_q_norm_g"], w["w_uq"], w["mla_kv_norm_g"], w["w_ukv"],
                   tabs_da, tabs_mla)


def _trunk(x, meta_kv, w, tabs_da, tabs_mla):
    nb, s, d = x.shape
    h1 = _ffn_ln(x.reshape(nb * s, d), w["ffn1_w_gate"], w["ffn1_w_up"], w["ffn1_w_down"], w["ln1_g"], w["ln1_b"])
    qdat, kda, vdat, qmt, km, vmt, knorm2 = _mixer_inputs(h1, nb, s, w, tabs_da, tabs_mla)
    kda_m, vdat_m, km_m, vmt_m, knorm2_m = meta_kv
    knorm2 = jnp.max(knorm2.reshape(nb, -1, DA_HEADS + MLA_HEADS, LANES), axis=1)
    kmax = jnp.sqrt(jnp.maximum(knorm2, knorm2_m))[:, :, None, :]
    lam_vecs = (w["da_lambda_q1"], w["da_lambda_k1"], w["da_lambda_q2"], w["da_lambda_k2"])
    oa = _da_attn(qdat, kda, vdat, kda_m, vdat_m, kmax[:, :DA_HEADS], lam_vecs, w["da_subln_g"])
    om = _mla_attn(qmt, km, vmt, km_m, vmt_m, kmax[:, DA_HEADS:])
    h2 = _outproj_ln(h1, oa.reshape(nb * s, DA_WIDTH), om.reshape(nb * s, MLA_WIDTH), w["w_o"], w["ln2_g"], w["ln2_b"])
    y = _ffn_ln(h2, w["ffn2_w_gate"], w["ffn2_w_up"], w["ffn2_w_down"], w["ln3_g"], w["ln3_b"])
    return y.reshape(nb, s, d)


def kernel(x_prompt, x_sample, meta_tokens, ffn1_w_gate, ffn1_w_up, ffn1_w_down, ln1_g, ln1_b, w_in, da_lambda_q1, da_lambda_k1, da_lambda_q2, da_lambda_k2, da_subln_g, mla_q_norm_g, w_uq, mla_kv_norm_g, w_ukv, w_o, ln2_g, ln2_b, ffn2_w_gate, ffn2_w_up, ffn2_w_down, ln3_g, ln3_b):
    p = dict(ffn1_w_gate=ffn1_w_gate, ffn1_w_up=ffn1_w_up, ffn1_w_down=ffn1_w_down, ln1_g=ln1_g, ln1_b=ln1_b,
             w_in=w_in, da_lambda_q1=da_lambda_q1, da_lambda_k1=da_lambda_k1, da_lambda_q2=da_lambda_q2,
             da_lambda_k2=da_lambda_k2, da_subln_g=da_subln_g, mla_q_norm_g=mla_q_norm_g, w_uq=w_uq,
             mla_kv_norm_g=mla_kv_norm_g, w_ukv=w_ukv, w_o=w_o, ln2_g=ln2_g, ln2_b=ln2_b,
             ffn2_w_gate=ffn2_w_gate, ffn2_w_up=ffn2_w_up, ffn2_w_down=ffn2_w_down, ln3_g=ln3_g, ln3_b=ln3_b)
    w = _prep_weights(p)
    s_max = max(x_prompt.shape[1], x_sample.shape[1])
    tabs_da = _rope_lane_tables(N_META + s_max, DA_ROT, DA_HEAD_DIM)
    tabs_mla = _rope_lane_tables(N_META + s_max, MLA_ROPE, LANES)

    hm = _ffn_ln(meta_tokens.astype(F32), w["ffn1_w_gate"], w["ffn1_w_up"], w["ffn1_w_down"], w["ln1_g"], w["ln1_b"])
    hm = jnp.pad(hm, ((0, PROJ_ROWS - N_META), (0, 0)))
    _, kda_m, vdat_m, _, km_m, vmt_m, knorm2_m = _mixer_inputs(
        hm, 1, PROJ_ROWS, w, [t[:PROJ_ROWS] for t in tabs_da], [t[:PROJ_ROWS] for t in tabs_mla])
    meta_kv = (kda_m[:, :, :META_PAD], vdat_m[:, :, 0, :, :META_PAD], km_m[:, :, :META_PAD],
               vmt_m[:, :, 0, :, :META_PAD], knorm2_m)

    seq_da = [t[N_META:] for t in tabs_da]
    seq_mla = [t[N_META:] for t in tabs_mla]
    y_prompt = _trunk(x_prompt, meta_kv, w, seq_da, seq_mla)
    y_sample = _trunk(x_sample, meta_kv, w, seq_da, seq_mla)
    return (y_prompt, y_sample)
```

```python
import functools
import math

import jax
import jax.numpy as jnp
from jax import lax
from jax.experimental import pallas as pl
from jax.experimental.pallas import tpu as pltpu

N_META = 16
ROPE_THETA = 500000.0
DA_HEADS = 8
DA_HEAD_DIM = 64
DA_V_DIM = 2 * DA_HEAD_DIM
DA_ROT = DA_HEAD_DIM // 4
DA_QK_COLS = DA_HEADS * 2 * DA_HEAD_DIM
DA_WIDTH = DA_HEADS * DA_V_DIM
MLA_HEADS = 8
MLA_Q_RANK = 512
MLA_KV_RANK = 256
MLA_NOPE = 128
MLA_ROPE = 64
MLA_V = 128
MLA_WIDTH = MLA_HEADS * MLA_V
MLA_QK_PAD = 256
DEPTH = 1
ALPHA = (2 * DEPTH) ** 0.25
LN_EPS = 1e-5
RMS_EPS = 1e-6
LAMBDA_INIT = 0.8 - 0.6 * math.exp(-0.3 * 0)
LOG2E = 1.4426950408889634

LANES = 128
SUBLANES = 8
BF16_SUBLANES = 16
MXU_COLS = 256
META_PAD = 128
NEG_BIG = -1e30

PROJ_ROWS = 256
DA_Q_TILE = 1024
MLA_Q_TILE = 2048
KV_CHUNK = 512
CHUNKS_PER_ITER = 8
ONLINE_CHUNKS_PER_ITER = 2
OUTPROJ_SUBBLOCKS = 2
SCORE_BOUND = 45.0
BOUND_SLACK = 1.01
V_EXTRA_ROWS = BF16_SUBLANES
V_ROWS = DA_V_DIM + V_EXTRA_ROWS

F32 = jnp.float32
BF16 = jnp.bfloat16
MIB = 1024 * 1024


def _cparams(n_axes, vmem_mib):
    return pltpu.CompilerParams(
        dimension_semantics=("arbitrary",) * n_axes,
        vmem_limit_bytes=int(vmem_mib * MIB),
    )


def _resident(shape):
    n = len(shape)
    return pl.BlockSpec(shape, lambda *_: (0,) * n, pipeline_mode=pl.Buffered(1))


def _layer_norm(y, g, b):
    mu = jnp.mean(y, axis=-1, keepdims=True)
    d = y - mu
    var = jnp.mean(d * d, axis=-1, keepdims=True)
    return d * lax.rsqrt(var + LN_EPS) * g + b


def _rms_norm(x, g):
    return x * lax.rsqrt(jnp.mean(x * x, axis=-1, keepdims=True) + RMS_EPS) * g


def _ffn_ln_kernel(x_ref, wg_ref, wu_ref, wd_ref, g_ref, b_ref, o_ref, xb_ref, acc_ref, y_ref, *, n_slices):
    i, j = pl.program_id(0), pl.program_id(1)
    n_blocks, n_chunks = pl.num_programs(0) - 1, pl.num_programs(1)
    slice_rows = y_ref.shape[0] // n_slices

    def norm_slice():
        r0 = pl.multiple_of(jnp.minimum(j, n_slices - 1) * slice_rows, slice_rows)
        rows = pl.ds(r0, slice_rows)
        o_ref[rows, :] = _layer_norm(y_ref[rows, :], g_ref[...], b_ref[...])

    @pl.when(i < n_blocks)
    def _():
        @pl.when(j == 0)
        def _():
            xb_ref[...] = x_ref[...].astype(BF16)
            acc_ref[...] = jnp.zeros_like(acc_ref)

        @pl.when((i == 0) & (j == 0))
        def _():
            y_ref[...] = jnp.zeros_like(y_ref)

        norm_slice()
        xb = xb_ref[...]
        tf = wg_ref.shape[1]
        half = tf // 2 if tf % (2 * MXU_COLS) == 0 else tf
        down = None
        for c0 in range(0, tf, half):
            gate = jnp.dot(xb, wg_ref[:, c0:c0 + half], preferred_element_type=F32)
            up = jnp.dot(xb, wu_ref[:, c0:c0 + half], preferred_element_type=F32)
            hid = (gate * (1.0 / (1.0 + jnp.exp(-gate)))) * up
            part = jnp.dot(hid.astype(BF16), wd_ref[c0:c0 + half, :], preferred_element_type=F32)
            down = part if down is None else down + part
        acc_ref[...] += down

        @pl.when(j == n_chunks - 1)
        def _():
            y_ref[...] = ALPHA * x_ref[...] + acc_ref[...]

    @pl.when(i == n_blocks)
    def _():
        norm_slice()


def _ffn_ln(x, wg, wu, wd, g, b, *, tm=512, tf=512):
    m, d = x.shape
    f = wg.shape[1]
    tm = min(tm, m)
    tf = min(tf, f)
    assert m % tm == 0 and f % tf == 0
    nb, nj = m // tm, f // tf
    n_slices = 1
    while 2 * n_slices <= min(nj, tm // SUBLANES):
        n_slices *= 2
    assert tm % (n_slices * SUBLANES) == 0

    def chunk(i, j):
        return jnp.where(i < nb, j, nj - 1)

    vmem = (2 * 2 * tm * d * 4 + 2 * 3 * d * tf * 2 + tm * d * 10 + 4 * tm * tf * 4) / MIB + 6
    return pl.pallas_call(
        functools.partial(_ffn_ln_kernel, n_slices=n_slices),
        grid=(nb + 1, nj),
        in_specs=[
            pl.BlockSpec((tm, d), lambda i, j: (jnp.minimum(i, nb - 1), 0)),
            pl.BlockSpec((d, tf), lambda i, j: (0, chunk(i, j))),
            pl.BlockSpec((d, tf), lambda i, j: (0, chunk(i, j))),
            pl.BlockSpec((tf, d), lambda i, j: (chunk(i, j), 0)),
            pl.BlockSpec((1, d), lambda i, j: (0, 0)),
            pl.BlockSpec((1, d), lambda i, j: (0, 0)),
        ],
        out_specs=pl.BlockSpec((tm, d), lambda i, j: (jnp.maximum(i - 1, 0), 0)),
        out_shape=jax.ShapeDtypeStruct((m, d), F32),
        scratch_shapes=[pltpu.VMEM((tm, d), BF16), pltpu.VMEM((tm, d), F32), pltpu.VMEM((tm, d), F32)],
        compiler_params=_cparams(2, vmem),
        name="ffn_ln",
    )(x, wg, wu, wd, g, b)


def _rope(x, c, sp, sm, half):
    return x * c + pltpu.roll(x, half, 1) * sp + pltpu.roll(x, LANES - half, 1) * sm


def _inproj_kernel(h_ref, win_ref, gq_ref, wuq_ref, gkv_ref, wukv_ref,
                   cda_ref, spda_ref, smda_ref, cm_ref, spm_ref, smm_ref,
                   qdat_ref, kda_ref, vdat_ref, qmt_ref, km_ref, vmt_ref, knorm_ref):
    tm = h_ref.shape[0]
    hb = h_ref[...].astype(BF16)

    def proj(lo, hi):
        return jnp.dot(hb, win_ref[:, lo:hi], preferred_element_type=F32)

    def row_sq_norm(x):
        return jnp.sum(x * x, axis=1, keepdims=True)

    def put_max_sq_norm(row, n2):
        knorm_ref[0, row:row + 1, :] = jnp.broadcast_to(jnp.max(n2, axis=0, keepdims=True), (1, LANES))

    cda, spda, smda = cda_ref[...], spda_ref[...], smda_ref[...]
    cm, spm, smm = cm_ref[...], spm_ref[...], smm_ref[...]
    da_half = DA_ROT // 2
    mla_half = MLA_ROPE // 2
    map1_dims = lax.broadcasted_iota(jnp.int32, (LANES, tm), 0) < DA_HEAD_DIM
    da_scale = DA_HEAD_DIM ** -0.5 * LOG2E
    mla_scale = (MLA_NOPE + MLA_ROPE) ** -0.5 * LOG2E
    sum_rows = (lax.broadcasted_iota(jnp.int32, (V_EXTRA_ROWS, tm), 0) == 0).astype(BF16)

    c0 = 2 * DA_QK_COLS + DA_WIDTH
    cq = _rms_norm(proj(c0, c0 + MLA_Q_RANK), gq_ref[...]).astype(BF16)
    c0 += MLA_Q_RANK
    qm = jnp.dot(cq, wuq_ref[...], preferred_element_type=F32) * mla_scale
    for h in range(MLA_HEADS):
        base = h * MLA_QK_PAD
        qmt_ref[0, h, 0, 0:MLA_NOPE, :] = qm[:, base:base + MLA_NOPE].T.astype(BF16)
        qmt_ref[0, h, 0, MLA_NOPE:MLA_QK_PAD, :] = _rope(
            qm[:, base + MLA_NOPE:base + MLA_QK_PAD], cm, spm, smm, mla_half).T.astype(BF16)

    ckv = _rms_norm(proj(c0, c0 + MLA_KV_RANK), gkv_ref[...]).astype(BF16)
    c0 += MLA_KV_RANK
    kv = jnp.dot(ckv, wukv_ref[...], preferred_element_type=F32)
    kr_f = _rope(proj(c0, c0 + LANES), cm, spm, smm, mla_half)
    kr = kr_f.astype(BF16)
    kr_n2 = row_sq_norm(kr_f)
    for h in range(MLA_HEADS):
        base = h * (MLA_NOPE + MLA_V)
        k_nope = kv[:, base:base + MLA_NOPE]
        km_ref[0, h, :, 0:MLA_NOPE] = k_nope.astype(BF16)
        km_ref[0, h, :, MLA_NOPE:MLA_QK_PAD] = kr
        put_max_sq_norm(DA_HEADS + h, row_sq_norm(k_nope) + kr_n2)
        vmt_ref[0, h, 0, 0:MLA_V, :] = kv[:, base + MLA_NOPE:base + MLA_NOPE + MLA_V].T.astype(BF16)
        vmt_ref[0, h, 0, MLA_V:V_ROWS, :] = sum_rows

    c0 = 0
    zq = proj(c0, c0 + DA_QK_COLS)
    for h in range(DA_HEADS):
        rt = (_rope(zq[:, h * LANES:(h + 1) * LANES], cda, spda, smda, da_half) * da_scale).T
        qdat_ref[0, h, 0, 0] = jnp.where(map1_dims, rt, 0.0).astype(BF16)
        qdat_ref[0, h, 0, 1] = jnp.where(map1_dims, 0.0, rt).astype(BF16)
    c0 += DA_QK_COLS
    zk = proj(c0, c0 + DA_QK_COLS)
    for h in range(DA_HEADS):
        k = _rope(zk[:, h * LANES:(h + 1) * LANES], cda, spda, smda, da_half)
        kda_ref[0, h] = k.astype(BF16)
        put_max_sq_norm(h, row_sq_norm(k))
    c0 += DA_QK_COLS
    zv = proj(c0, c0 + DA_WIDTH)
    for h in range(DA_HEADS):
        vdat_ref[0, h, 0, 0:DA_V_DIM, :] = zv[:, h * DA_V_DIM:(h + 1) * DA_V_DIM].T.astype(BF16)
        vdat_ref[0, h, 0, DA_V_DIM:V_ROWS, :] = sum_rows


def _inproj(h, nb, s, win, gq, wuq, gkv, wukv, tabs_da, tabs_mla):
    m, d = h.shape
    tm = PROJ_ROWS
    ck = min(KV_CHUNK, s)
    tqd = min(DA_Q_TILE, s)
    tqm = min(MLA_Q_TILE, s)
    assert m == nb * s and all(t % tm == 0 and s % t == 0 for t in (tm, ck, tqd, tqm))
    spb = s // tm
    tab_spec = pl.BlockSpec((tm, LANES), lambda i: (i % spb, 0))

    def rows_spec(width):
        return pl.BlockSpec((1, DA_HEADS, tm, width), lambda i: (i // spb, 0, i % spb, 0))

    def cols_spec(rows, tile, *mid):
        per = tile // tm
        zeros = (0,) * (len(mid) + 1)
        return pl.BlockSpec((1, DA_HEADS, 1) + mid + (rows, tm),
                            lambda i: (i // spb, 0, (i % spb) // per) + zeros + ((i % spb) % per,))

    def shape(*tail):
        return jax.ShapeDtypeStruct((nb, DA_HEADS) + tail, BF16)

    vmem = (2 * tm * d * 4 + (win.size + wuq.size + wukv.size) * 2 + 2 * 6 * tm * LANES * 4
            + 2 * tm * 8 * (256 + 128 + 128 + 256 + 256 + 128) * 2 + 8 * tm * 2048 * 4) / MIB + 6
    return pl.pallas_call(
        _inproj_kernel,
        grid=(m // tm,),
        in_specs=[
            pl.BlockSpec((tm, d), lambda i: (i, 0)),
            _resident(win.shape), _resident(gq.shape), _resident(wuq.shape),
            _resident(gkv.shape), _resident(wukv.shape),
            tab_spec, tab_spec, tab_spec, tab_spec, tab_spec, tab_spec,
        ],
        out_specs=[
            cols_spec(LANES, tqd, 2),
            rows_spec(LANES), cols_spec(V_ROWS, ck),
            cols_spec(MLA_QK_PAD, tqm), rows_spec(MLA_QK_PAD), cols_spec(V_ROWS, ck),
            pl.BlockSpec((1, DA_HEADS + MLA_HEADS, LANES), lambda i: (i, 0, 0)),
        ],
        out_shape=[
            shape(s // tqd, 2, LANES, tqd), shape(s, LANES), shape(s // ck, V_ROWS, ck),
            shape(s // tqm, MLA_QK_PAD, tqm), shape(s, MLA_QK_PAD), shape(s // ck, V_ROWS, ck),
            jax.ShapeDtypeStruct((m // tm, DA_HEADS + MLA_HEADS, LANES), F32),
        ],
        compiler_params=_cparams(1, vmem),
        name="inproj",
    )(h, win, gq, wuq, gkv, wukv, *tabs_da, *tabs_mla)


def _masked_meta_scores(kmeta, qt):
    s = jnp.dot(kmeta, qt, preferred_element_type=F32)
    return jnp.where(lax.broadcasted_iota(jnp.int32, s.shape, 0) < N_META, s, NEG_BIG)


def _score_shift(qt, kmax):
    qf = qt.astype(F32)
    return jnp.sqrt(jnp.sum(qf * qf, axis=0, keepdims=True)) * (kmax[0:1, 0:1] * BOUND_SLACK)


def _flash_cols_shifted(qt, s_meta, shift, k_ref, vt_ref, vmeta_t, acc_scr):
    n_chunks, tk = vt_ref.shape[2], vt_ref.shape[4]
    group = CHUNKS_PER_ITER

    def weights(s):
        return jnp.exp2(s - shift).astype(BF16)

    def chunk_pv(c):
        off = c * tk if isinstance(c, int) else pl.multiple_of(c * tk, tk)
        s = jnp.dot(k_ref[0, 0, pl.ds(off, tk), :], qt, preferred_element_type=F32)
        return jnp.dot(vt_ref[0, 0, c], weights(s), preferred_element_type=F32)

    acc_scr[...] = jnp.zeros_like(acc_scr)

    def body(j, carry):
        total = chunk_pv(group * j)
        for u in range(1, group):
            total = total + chunk_pv(group * j + u)
        acc_scr[...] += total
        return carry

    n_iter = n_chunks // group if n_chunks >= 2 * group else 0
    lax.fori_loop(0, n_iter, body, 0)
    for c in range(group * n_iter, n_chunks):
        acc_scr[...] += chunk_pv(c)
    return acc_scr[...] + jnp.dot(vmeta_t, weights(s_meta), preferred_element_type=F32)


def _flash_cols_online(qt, s_meta, k_ref, vt_ref, vmeta_t, s_scrs, cmax_scrs, m_scr, acc_scr):
    n_chunks, tk = vt_ref.shape[2], vt_ref.shape[4]
    r = qt.shape[1]
    group = ONLINE_CHUNKS_PER_ITER
    assert group % 2 == 0

    def bcast(x):
        return jnp.broadcast_to(x, (SUBLANES, r))

    def scores(c):
        off = c * tk if isinstance(c, int) else pl.multiple_of(c * tk, tk)
        return jnp.dot(k_ref[0, 0, pl.ds(off, tk), :], qt, preferred_element_type=F32)

    def put(s, slot, rows):
        s_scrs[slot][0:rows, :] = s
        cmax_scrs[slot][...] = bcast(jnp.max(s, axis=0, keepdims=True))

    def fold(slot, rows, vt, first=False):
        cmax = cmax_scrs[slot][0:1, :]
        m_new = cmax if first else jnp.maximum(m_scr[0:1, :], cmax)
        p = jnp.exp2(s_scrs[slot][0:rows, :] - m_new)
        pv = jnp.dot(vt, p.astype(BF16), preferred_element_type=F32)
        if first:
            acc_scr[...] = pv
        else:
            acc_scr[...] = jnp.exp2(m_scr[0:1, :] - m_new) * acc_scr[...] + pv
        m_scr[...] = bcast(m_new)

    put(s_meta, 1, META_PAD)
    put(scores(0), 0, tk)
    fold(1, META_PAD, vmeta_t, first=True)

    def step(c, parity, last):
        if not last:
            put(scores(c + 1), 1 - parity, tk)
        fold(parity, tk, vt_ref[0, 0, c])

    n_iter = (n_chunks - 1) // group

    def body(j, carry):
        for u in range(group):
            step(group * j + u, u % 2, False)
        return carry

    lax.fori_loop(0, n_iter, body, 0)
    for c in range(group * n_iter, n_chunks):
        step(c, c % 2, c == n_chunks - 1)
    return acc_scr[...]


def _normalised(acc, dv):
    return acc[0:dv, :] * (1.0 / acc[dv:dv + 1, :])


def _flash_cols(qt, k_ref, vt_ref, kmeta, vmeta_t, kmax, finish,
                s0_scr, s1_scr, c0_scr, c1_scr, m_scr, acc_scr):
    s_meta = _masked_meta_scores(kmeta, qt)
    shift = _score_shift(qt, kmax)
    bounded = jnp.max(shift) <= SCORE_BOUND

    @pl.when(bounded)
    def _():
        finish(_flash_cols_shifted(qt, s_meta, shift, k_ref, vt_ref, vmeta_t, acc_scr))

    @pl.when(jnp.logical_not(bounded))
    def _():
        finish(_flash_cols_online(qt, s_meta, k_ref, vt_ref, vmeta_t,
                                  (s0_scr, s1_scr), (c0_scr, c1_scr), m_scr, acc_scr))


def _da_attn_kernel(qt_ref, k_ref, vt_ref, kmeta_ref, vmeta_ref, kmax_ref, lq1_ref, lk1_ref, lq2_ref, lk2_ref,
                    g_ref, o_ref, *scratch):
    tq = o_ref.shape[1]
    qt = jnp.concatenate([qt_ref[0, 0, 0, 0], qt_ref[0, 0, 0, 1]], axis=1)

    def finish(acc):
        at = _normalised(acc, DA_V_DIM)
        lam = (jnp.exp(jnp.sum(lq1_ref[...] * lk1_ref[...], axis=1, keepdims=True))
               - jnp.exp(jnp.sum(lq2_ref[...] * lk2_ref[...], axis=1, keepdims=True)) + LAMBDA_INIT)
        ot = at[:, :tq] - lam * at[:, tq:]
        inv = lax.rsqrt(jnp.mean(ot * ot, axis=0, keepdims=True) + RMS_EPS)
        o_ref[0] = ((ot * inv).T * g_ref[...] * (1.0 - LAMBDA_INIT)).astype(BF16)

    _flash_cols(qt, k_ref, vt_ref, kmeta_ref[0, 0], vmeta_ref[0, 0], kmax_ref[0, 0], finish, *scratch)


def _mla_attn_kernel(qt_ref, k_ref, vt_ref, kmeta_ref, vmeta_ref, kmax_ref, o_ref, *scratch):
    def finish(acc):
        o_ref[0] = _normalised(acc, MLA_V).T.astype(BF16)

    _flash_cols(qt_ref[0, 0, 0], k_ref, vt_ref, kmeta_ref[0, 0], vmeta_ref[0, 0], kmax_ref[0, 0], finish, *scratch)


def _attn_scratch(cols, tk):
    stat = pltpu.VMEM((SUBLANES, cols), F32)
    return [pltpu.VMEM((tk, cols), F32), pltpu.VMEM((tk, cols), F32), stat, stat, stat,
            pltpu.VMEM((V_ROWS, cols), F32)]


def _kmax_spec():
    return pl.BlockSpec((1, 1, 1, LANES), lambda b, h, i: (b, h, 0, 0))


def _attn_vmem(cols, tk, s, dk, dv):
    return (2 * s * (dk + dv) * 2 + 2 * dk * cols * 2 + 2 * tk * cols * 4 + dv * cols * 4
            + 4 * tk * cols * 4 + 2 * cols * dv * 2) / MIB + 8


def _da_attn(qdat, kda, vdat, kmeta, vmeta_t, kmax, lam_vecs, g):
    nb, nh, nq, _, _, tq = qdat.shape
    cols = 2 * tq
    s = kda.shape[2]
    nc, tk = vdat.shape[2], vdat.shape[4]
    vec_spec = pl.BlockSpec((1, DA_HEAD_DIM), lambda b, h, i: (0, 0))
    meta_spec = pl.BlockSpec((1, 1, META_PAD, LANES), lambda b, h, i: (0, h, 0, 0))
    vmeta_spec = pl.BlockSpec((1, 1, V_ROWS, META_PAD), lambda b, h, i: (0, h, 0, 0))
    return pl.pallas_call(
        _da_attn_kernel,
        grid=(nb, nh, nq),
        in_specs=[
            pl.BlockSpec((1, 1, 1, 2, LANES, tq), lambda b, h, i: (b, h, i, 0, 0, 0)),
            pl.BlockSpec((1, 1, s, LANES), lambda b, h, i: (b, h, 0, 0)),
            pl.BlockSpec((1, 1, nc, V_ROWS, tk), lambda b, h, i: (b, h, 0, 0, 0)),
            meta_spec, vmeta_spec, _kmax_spec(),
            vec_spec, vec_spec, vec_spec, vec_spec,
            pl.BlockSpec((1, DA_V_DIM), lambda b, h, i: (0, 0)),
        ],
        out_specs=pl.BlockSpec((1, tq, DA_V_DIM), lambda b, h, i: (b, i, h)),
        out_shape=jax.ShapeDtypeStruct((nb, s, DA_WIDTH), BF16),
        scratch_shapes=_attn_scratch(cols, tk),
        compiler_params=_cparams(3, _attn_vmem(cols, tk, s, LANES, DA_V_DIM)),
        name="da_attn",
    )(qdat, kda, vdat, kmeta, vmeta_t, kmax, *lam_vecs, g)


def _mla_attn(qmt, km, vmt, kmeta, vmeta_t, kmax):
    nb, nh, nq, _, tq = qmt.shape
    s = km.shape[2]
    nc, tk = vmt.shape[2], vmt.shape[4]
    return pl.pallas_call(
        _mla_attn_kernel,
        grid=(nb, nh, nq),
        in_specs=[
            pl.BlockSpec((1, 1, 1, MLA_QK_PAD, tq), lambda b, h, i: (b, h, i, 0, 0)),
            pl.BlockSpec((1, 1, s, MLA_QK_PAD), lambda b, h, i: (b, h, 0, 0)),
            pl.BlockSpec((1, 1, nc, V_ROWS, tk), lambda b, h, i: (b, h, 0, 0, 0)),
            pl.BlockSpec((1, 1, META_PAD, MLA_QK_PAD), lambda b, h, i: (0, h, 0, 0)),
            pl.BlockSpec((1, 1, V_ROWS, META_PAD), lambda b, h, i: (0, h, 0, 0)),
            _kmax_spec(),
        ],
        out_specs=pl.BlockSpec((1, tq, MLA_V), lambda b, h, i: (b, i, h)),
        out_shape=jax.ShapeDtypeStruct((nb, s, MLA_WIDTH), BF16),
        scratch_shapes=_attn_scratch(tq, tk),
        compiler_params=_cparams(3, _attn_vmem(tq, tk, s, MLA_QK_PAD, MLA_V)),
        name="mla_attn",
    )(qmt, km, vmt, kmeta, vmeta_t, kmax)


def _outproj_ln_kernel(h_ref, oa_ref, om_ref, wo_ref, g_ref, b_ref, o_ref):
    tm = h_ref.shape[0]
    sub = tm // OUTPROJ_SUBBLOCKS if tm % (OUTPROJ_SUBBLOCKS * BF16_SUBLANES) == 0 else tm
    for r0 in range(0, tm, sub):
        rows = pl.ds(r0, sub)
        mix = (jnp.dot(oa_ref[rows, :], wo_ref[0:DA_WIDTH, :], preferred_element_type=F32)
               + jnp.dot(om_ref[rows, :], wo_ref[DA_WIDTH:DA_WIDTH + MLA_WIDTH, :], preferred_element_type=F32))
        o_ref[rows, :] = _layer_norm(ALPHA * h_ref[rows, :] + mix, g_ref[...], b_ref[...])


def _outproj_ln(h, oa, om, wo, g, b, *, tm=512):
    m, d = h.shape
    tm = min(tm, m)
    assert m % tm == 0
    vmem = (2 * 2 * tm * d * 4 + 2 * 2 * tm * DA_WIDTH * 2 + wo.size * 2 + 3 * tm * d * 4) / MIB + 6
    return pl.pallas_call(
        _outproj_ln_kernel,
        grid=(m // tm,),
        in_specs=[
            pl.BlockSpec((tm, d), lambda i: (i, 0)),
            pl.BlockSpec((tm, DA_WIDTH), lambda i: (i, 0)),
            pl.BlockSpec((tm, MLA_WIDTH), lambda i: (i, 0)),
            _resident(wo.shape), _resident(g.shape), _resident(b.shape),
        ],
        out_specs=pl.BlockSpec((tm, d), lambda i: (i, 0)),
        out_shape=jax.ShapeDtypeStruct((m, d), F32),
        compiler_params=_cparams(1, vmem),
        name="outproj_ln",
    )(h, oa, om, wo, g, b)


def _rope_lane_tables(n_pos, rot_dim, period):
    half = rot_dim // 2
    inv = 1.0 / (ROPE_THETA ** (jnp.arange(0, rot_dim, 2, dtype=F32) / rot_dim))
    ang = jnp.arange(n_pos, dtype=F32)[:, None] * inv[None, :]
    cos, sin = jnp.cos(ang), jnp.sin(ang)
    within = jnp.arange(LANES) % period
    idx = within % half
    c = jnp.where(within < rot_dim, cos[:, idx], 1.0)
    sp = jnp.where((within >= half) & (within < rot_dim), sin[:, idx], 0.0)
    sm = jnp.where(within < half, -sin[:, idx], 0.0)
    return c, sp, sm


def _prep_weights(p):
    w = {}
    for name in ("ffn1_w_gate", "ffn1_w_up", "ffn2_w_gate", "ffn2_w_up", "w_o", "w_ukv"):
        w[name] = p[name][0].astype(BF16)
    for name in ("ffn1_w_down", "ffn2_w_down"):
        w[name] = (p[name][0] * 0.5).astype(BF16)
    w_in = p["w_in"][0]
    w["w_in"] = jnp.pad(w_in, ((0, 0), (0, LANES - MLA_ROPE))).astype(BF16)
    w_uq = p["w_uq"][0].reshape(MLA_Q_RANK, MLA_HEADS, MLA_NOPE + MLA_ROPE)
    w_uq = jnp.pad(w_uq, ((0, 0), (0, 0), (0, MLA_QK_PAD - MLA_NOPE - MLA_ROPE)))
    w["w_uq"] = w_uq.reshape(MLA_Q_RANK, MLA_HEADS * MLA_QK_PAD).astype(BF16)
    for name in ("ln1_g", "ln1_b", "ln2_g", "ln2_b", "ln3_g", "ln3_b", "da_subln_g", "mla_q_norm_g", "mla_kv_norm_g",
                 "da_lambda_q1", "da_lambda_k1", "da_lambda_q2", "da_lambda_k2"):
        w[name] = p[name].astype(F32)
    return w


def _mixer_inputs(h, nb, s, w, tabs_da, tabs_mla):
    return _inproj(h, nb, s, w["w_in"], w["mla_q_norm_g"], w["w_uq"], w["mla_kv_norm_g"], w["w_ukv"],
                   tabs_da, tabs_mla)


def _trunk(x, meta_kv, w, tabs_da, tabs_mla):
    nb, s, d = x.shape
    h1 = _ffn_ln(x.reshape(nb * s, d), w["ffn1_w_gate"], w["ffn1_w_up"], w["ffn1_w_down"], w["ln1_g"], w["ln1_b"])
    qdat, kda, vdat, qmt, km, vmt, knorm2 = _mixer_inputs(h1, nb, s, w, tabs_da, tabs_mla)
    kda_m, vdat_m, km_m, vmt_m, knorm2_m = meta_kv
    knorm2 = jnp.max(knorm2.reshape(nb, -1, DA_HEADS + MLA_HEADS, LANES), axis=1)
    kmax = jnp.sqrt(jnp.maximum(knorm2, knorm2_m))[:, :, None, :]
    lam_vecs = (w["da_lambda_q1"], w["da_lambda_k1"], w["da_lambda_q2"], w["da_lambda_k2"])
    oa = _da_attn(qdat, kda, vdat, kda_m, vdat_m, kmax[:, :DA_HEADS], lam_vecs, w["da_subln_g"])
    om = _mla_attn(qmt, km, vmt, km_m, vmt_m, kmax[:, DA_HEADS:])
    h2 = _outproj_ln(h1, oa.reshape(nb * s, DA_WIDTH), om.reshape(nb * s, MLA_WIDTH), w["w_o"], w["ln2_g"], w["ln2_b"])
    y = _ffn_ln(h2, w["ffn2_w_gate"], w["ffn2_w_up"], w["ffn2_w_down"], w["ln3_g"], w["ln3_b"])
    return y.reshape(nb, s, d)


def kernel(x_prompt, x_sample, meta_tokens, ffn1_w_gate, ffn1_w_up, ffn1_w_down, ln1_g, ln1_b, w_in, da_lambda_q1, da_lambda_k1, da_lambda_q2, da_lambda_k2, da_subln_g, mla_q_norm_g, w_uq, mla_kv_norm_g, w_ukv, w_o, ln2_g, ln2_b, ffn2_w_gate, ffn2_w_up, ffn2_w_down, ln3_g, ln3_b):
    p = dict(ffn1_w_gate=ffn1_w_gate, ffn1_w_up=ffn1_w_up, ffn1_w_down=ffn1_w_down, ln1_g=ln1_g, ln1_b=ln1_b,
             w_in=w_in, da_lambda_q1=da_lambda_q1, da_lambda_k1=da_lambda_k1, da_lambda_q2=da_lambda_q2,
             da_lambda_k2=da_lambda_k2, da_subln_g=da_subln_g, mla_q_norm_g=mla_q_norm_g, w_uq=w_uq,
             mla_kv_norm_g=mla_kv_norm_g, w_ukv=w_ukv, w_o=w_o, ln2_g=ln2_g, ln2_b=ln2_b,
             ffn2_w_gate=ffn2_w_gate, ffn2_w_up=ffn2_w_up, ffn2_w_down=ffn2_w_down, ln3_g=ln3_g, ln3_b=ln3_b)
    w = _prep_weights(p)
    s_max = max(x_prompt.shape[1], x_sample.shape[1])
    tabs_da = _rope_lane_tables(N_META + s_max, DA_ROT, DA_HEAD_DIM)
    tabs_mla = _rope_lane_tables(N_META + s_max, MLA_ROPE, LANES)

    hm = _ffn_ln(meta_tokens.astype(F32), w["ffn1_w_gate"], w["ffn1_w_up"], w["ffn1_w_down"], w["ln1_g"], w["ln1_b"])
    hm = jnp.pad(hm, ((0, PROJ_ROWS - N_META), (0, 0)))
    _, kda_m, vdat_m, _, km_m, vmt_m, knorm2_m = _mixer_inputs(
        hm, 1, PROJ_ROWS, w, [t[:PROJ_ROWS] for t in tabs_da], [t[:PROJ_ROWS] for t in tabs_mla])
    meta_kv = (kda_m[:, :, :META_PAD], vdat_m[:, :, 0, :, :META_PAD], km_m[:, :, :META_PAD],
               vmt_m[:, :, 0, :, :META_PAD], knorm2_m)

    seq_da = [t[N_META:] for t in tabs_da]
    seq_mla = [t[N_META:] for t in tabs_mla]
    y_prompt = _trunk(x_prompt, meta_kv, w, seq_da, seq_mla)
    y_sample = _trunk(x_sample, meta_kv, w, seq_da, seq_mla)
    return (y_prompt, y_sample)
```

```python
import functools
import math

import jax
import jax.numpy as jnp
from jax import lax
from jax.experimental import pallas as pl
from jax.experimental.pallas import tpu as pltpu

N_META = 16
ROPE_THETA = 500000.0
DA_HEADS = 8
DA_HEAD_DIM = 64
DA_V_DIM = 2 * DA_HEAD_DIM
DA_ROT = DA_HEAD_DIM // 4
DA_QK_COLS = DA_HEADS * 2 * DA_HEAD_DIM
DA_WIDTH = DA_HEADS * DA_V_DIM
MLA_HEADS = 8
MLA_Q_RANK = 512
MLA_KV_RANK = 256
MLA_NOPE = 128
MLA_ROPE = 64
MLA_V = 128
MLA_WIDTH = MLA_HEADS * MLA_V
MLA_QK_PAD = 256
DEPTH = 1
ALPHA = (2 * DEPTH) ** 0.25
LN_EPS = 1e-5
RMS_EPS = 1e-6
LAMBDA_INIT = 0.8 - 0.6 * math.exp(-0.3 * 0)
LOG2E = 1.4426950408889634

LANES = 128
SUBLANES = 8
BF16_SUBLANES = 16
META_PAD = 128
NEG_BIG = -1e30

PROJ_ROWS = 256
DA_Q_TILE = 1024
MLA_Q_TILE = 2048
KV_CHUNK = 512
CHUNKS_PER_ITER = 8
ONLINE_CHUNKS_PER_ITER = 2
OUTPROJ_SUBBLOCKS = 2
SCORE_BOUND = 45.0
BOUND_SLACK = 1.01
V_EXTRA_ROWS = BF16_SUBLANES
V_ROWS = DA_V_DIM + V_EXTRA_ROWS

F32 = jnp.float32
BF16 = jnp.bfloat16
MIB = 1024 * 1024


def _cparams(n_axes, vmem_mib):
    return pltpu.CompilerParams(
        dimension_semantics=("arbitrary",) * n_axes,
        vmem_limit_bytes=int(vmem_mib * MIB),
    )


def _resident(shape):
    n = len(shape)
    return pl.BlockSpec(shape, lambda *_: (0,) * n, pipeline_mode=pl.Buffered(1))


def _layer_norm(y, g, b):
    mu = jnp.mean(y, axis=-1, keepdims=True)
    d = y - mu
    var = jnp.mean(d * d, axis=-1, keepdims=True)
    return d * lax.rsqrt(var + LN_EPS) * g + b


def _rms_norm(x, g):
    return x * lax.rsqrt(jnp.mean(x * x, axis=-1, keepdims=True) + RMS_EPS) * g


def _ffn_ln_kernel(x_ref, wg_ref, wu_ref, wd_ref, g_ref, b_ref, o_ref, xb_ref, acc_ref, y_ref, *, n_slices):
    i, j = pl.program_id(0), pl.program_id(1)
    n_blocks, n_chunks = pl.num_programs(0) - 1, pl.num_programs(1)
    slice_rows = y_ref.shape[0] // n_slices

    def norm_slice():
        r0 = pl.multiple_of(jnp.minimum(j, n_slices - 1) * slice_rows, slice_rows)
        rows = pl.ds(r0, slice_rows)
        o_ref[rows, :] = _layer_norm(y_ref[rows, :], g_ref[...], b_ref[...])

    @pl.when(i < n_blocks)
    def _():
        @pl.when(j == 0)
        def _():
            xb_ref[...] = x_ref[...].astype(BF16)
            acc_ref[...] = jnp.zeros_like(acc_ref)

        @pl.when((i == 0) & (j == 0))
        def _():
            y_ref[...] = jnp.zeros_like(y_ref)

        norm_slice()
        xb = xb_ref[...]
        gate = jnp.dot(xb, wg_ref[...], preferred_element_type=F32)
        up = jnp.dot(xb, wu_ref[...], preferred_element_type=F32)
        hid = (gate * (1.0 / (1.0 + jnp.exp(-gate)))) * up
        acc_ref[...] += jnp.dot(hid.astype(BF16), wd_ref[...], preferred_element_type=F32)

        @pl.when(j == n_chunks - 1)
        def _():
            y_ref[...] = ALPHA * x_ref[...] + acc_ref[...]

    @pl.when(i == n_blocks)
    def _():
        norm_slice()


def _ffn_ln(x, wg, wu, wd, g, b, *, tm=512, tf=512):
    m, d = x.shape
    f = wg.shape[1]
    tm = min(tm, m)
    tf = min(tf, f)
    assert m % tm == 0 and f % tf == 0
    nb, nj = m // tm, f // tf
    n_slices = 1
    while 2 * n_slices <= min(nj, tm // SUBLANES):
        n_slices *= 2
    assert tm % (n_slices * SUBLANES) == 0

    def chunk(i, j):
        return jnp.where(i < nb, j, nj - 1)

    vmem = (2 * 2 * tm * d * 4 + 2 * 3 * d * tf * 2 + tm * d * 10 + 4 * tm * tf * 4) / MIB + 6
    return pl.pallas_call(
        functools.partial(_ffn_ln_kernel, n_slices=n_slices),
        grid=(nb + 1, nj),
        in_specs=[
            pl.BlockSpec((tm, d), lambda i, j: (jnp.minimum(i, nb - 1), 0)),
            pl.BlockSpec((d, tf), lambda i, j: (0, chunk(i, j))),
            pl.BlockSpec((d, tf), lambda i, j: (0, chunk(i, j))),
            pl.BlockSpec((tf, d), lambda i, j: (chunk(i, j), 0)),
            pl.BlockSpec((1, d), lambda i, j: (0, 0)),
            pl.BlockSpec((1, d), lambda i, j: (0, 0)),
        ],
        out_specs=pl.BlockSpec((tm, d), lambda i, j: (jnp.maximum(i - 1, 0), 0)),
        out_shape=jax.ShapeDtypeStruct((m, d), F32),
        scratch_shapes=[pltpu.VMEM((tm, d), BF16), pltpu.VMEM((tm, d), F32), pltpu.VMEM((tm, d), F32)],
        compiler_params=_cparams(2, vmem),
        name="ffn_ln",
    )(x, wg, wu, wd, g, b)


def _rope(x, c, sp, sm, half):
    return x * c + pltpu.roll(x, half, 1) * sp + pltpu.roll(x, LANES - half, 1) * sm


def _inproj_kernel(h_ref, win_ref, gq_ref, wuq_ref, gkv_ref, wukv_ref,
                   cda_ref, spda_ref, smda_ref, cm_ref, spm_ref, smm_ref,
                   qdat_ref, kda_ref, vdat_ref, qmt_ref, km_ref, vmt_ref, knorm_ref):
    tm = h_ref.shape[0]
    hb = h_ref[...].astype(BF16)

    def proj(lo, hi):
        return jnp.dot(hb, win_ref[:, lo:hi], preferred_element_type=F32)

    def row_sq_norm(x):
        return jnp.sum(x * x, axis=1, keepdims=True)

    def put_max_sq_norm(row, n2):
        knorm_ref[0, row:row + 1, :] = jnp.broadcast_to(jnp.max(n2, axis=0, keepdims=True), (1, LANES))

    cda, spda, smda = cda_ref[...], spda_ref[...], smda_ref[...]
    cm, spm, smm = cm_ref[...], spm_ref[...], smm_ref[...]
    da_half = DA_ROT // 2
    mla_half = MLA_ROPE // 2
    map1_dims = lax.broadcasted_iota(jnp.int32, (LANES, tm), 0) < DA_HEAD_DIM
    da_scale = DA_HEAD_DIM ** -0.5 * LOG2E
    mla_scale = (MLA_NOPE + MLA_ROPE) ** -0.5 * LOG2E
    sum_rows = (lax.broadcasted_iota(jnp.int32, (V_EXTRA_ROWS, tm), 0) == 0).astype(BF16)

    c0 = 2 * DA_QK_COLS + DA_WIDTH
    cq = _rms_norm(proj(c0, c0 + MLA_Q_RANK), gq_ref[...]).astype(BF16)
    c0 += MLA_Q_RANK
    qm = jnp.dot(cq, wuq_ref[...], preferred_element_type=F32) * mla_scale
    for h in range(MLA_HEADS):
        base = h * MLA_QK_PAD
        qmt_ref[0, h, 0, 0:MLA_NOPE, :] = qm[:, base:base + MLA_NOPE].T.astype(BF16)
        qmt_ref[0, h, 0, MLA_NOPE:MLA_QK_PAD, :] = _rope(
            qm[:, base + MLA_NOPE:base + MLA_QK_PAD], cm, spm, smm, mla_half).T.astype(BF16)

    ckv = _rms_norm(proj(c0, c0 + MLA_KV_RANK), gkv_ref[...]).astype(BF16)
    c0 += MLA_KV_RANK
    kv = jnp.dot(ckv, wukv_ref[...], preferred_element_type=F32)
    kr_f = _rope(proj(c0, c0 + LANES), cm, spm, smm, mla_half)
    kr = kr_f.astype(BF16)
    kr_n2 = row_sq_norm(kr_f)
    for h in range(MLA_HEADS):
        base = h * (MLA_NOPE + MLA_V)
        k_nope = kv[:, base:base + MLA_NOPE]
        km_ref[0, h, :, 0:MLA_NOPE] = k_nope.astype(BF16)
        km_ref[0, h, :, MLA_NOPE:MLA_QK_PAD] = kr
        put_max_sq_norm(DA_HEADS + h, row_sq_norm(k_nope) + kr_n2)
        vmt_ref[0, h, 0, 0:MLA_V, :] = kv[:, base + MLA_NOPE:base + MLA_NOPE + MLA_V].T.astype(BF16)
        vmt_ref[0, h, 0, MLA_V:V_ROWS, :] = sum_rows

    c0 = 0
    zq = proj(c0, c0 + DA_QK_COLS)
    for h in range(DA_HEADS):
        rt = (_rope(zq[:, h * LANES:(h + 1) * LANES], cda, spda, smda, da_half) * da_scale).T
        qdat_ref[0, h, 0, 0] = jnp.where(map1_dims, rt, 0.0).astype(BF16)
        qdat_ref[0, h, 0, 1] = jnp.where(map1_dims, 0.0, rt).astype(BF16)
    c0 += DA_QK_COLS
    zk = proj(c0, c0 + DA_QK_COLS)
    for h in range(DA_HEADS):
        k = _rope(zk[:, h * LANES:(h + 1) * LANES], cda, spda, smda, da_half)
        kda_ref[0, h] = k.astype(BF16)
        put_max_sq_norm(h, row_sq_norm(k))
    c0 += DA_QK_COLS
    zv = proj(c0, c0 + DA_WIDTH)
    for h in range(DA_HEADS):
        vdat_ref[0, h, 0, 0:DA_V_DIM, :] = zv[:, h * DA_V_DIM:(h + 1) * DA_V_DIM].T.astype(BF16)
        vdat_ref[0, h, 0, DA_V_DIM:V_ROWS, :] = sum_rows


def _inproj(h, nb, s, win, gq, wuq, gkv, wukv, tabs_da, tabs_mla):
    m, d = h.shape
    tm = PROJ_ROWS
    ck = min(KV_CHUNK, s)
    tqd = min(DA_Q_TILE, s)
    tqm = min(MLA_Q_TILE, s)
    assert m == nb * s and all(t % tm == 0 and s % t == 0 for t in (tm, ck, tqd, tqm))
    spb = s // tm
    tab_spec = pl.BlockSpec((tm, LANES), lambda i: (i % spb, 0))

    def rows_spec(width):
        return pl.BlockSpec((1, DA_HEADS, tm, width), lambda i: (i // spb, 0, i % spb, 0))

    def cols_spec(rows, tile, *mid):
        per = tile // tm
        zeros = (0,) * (len(mid) + 1)
        return pl.BlockSpec((1, DA_HEADS, 1) + mid + (rows, tm),
                            lambda i: (i // spb, 0, (i % spb) // per) + zeros + ((i % spb) % per,))

    def shape(*tail):
        return jax.ShapeDtypeStruct((nb, DA_HEADS) + tail, BF16)

    vmem = (2 * tm * d * 4 + (win.size + wuq.size + wukv.size) * 2 + 2 * 6 * tm * LANES * 4
            + 2 * tm * 8 * (256 + 128 + 128 + 256 + 256 + 128) * 2 + 8 * tm * 2048 * 4) / MIB + 6
    return pl.pallas_call(
        _inproj_kernel,
        grid=(m // tm,),
        in_specs=[
            pl.BlockSpec((tm, d), lambda i: (i, 0)),
            _resident(win.shape), _resident(gq.shape), _resident(wuq.shape),
            _resident(gkv.shape), _resident(wukv.shape),
            tab_spec, tab_spec, tab_spec, tab_spec, tab_spec, tab_spec,
        ],
        out_specs=[
            cols_spec(LANES, tqd, 2),
            rows_spec(LANES), cols_spec(V_ROWS, ck),
            cols_spec(MLA_QK_PAD, tqm), rows_spec(MLA_QK_PAD), cols_spec(V_ROWS, ck),
            pl.BlockSpec((1, DA_HEADS + MLA_HEADS, LANES), lambda i: (i, 0, 0)),
        ],
        out_shape=[
            shape(s // tqd, 2, LANES, tqd), shape(s, LANES), shape(s // ck, V_ROWS, ck),
            shape(s // tqm, MLA_QK_PAD, tqm), shape(s, MLA_QK_PAD), shape(s // ck, V_ROWS, ck),
            jax.ShapeDtypeStruct((m // tm, DA_HEADS + MLA_HEADS, LANES), F32),
        ],
        compiler_params=_cparams(1, vmem),
        name="inproj",
    )(h, win, gq, wuq, gkv, wukv, *tabs_da, *tabs_mla)


def _masked_meta_scores(kmeta, qt):
    s = jnp.dot(kmeta, qt, preferred_element_type=F32)
    return jnp.where(lax.broadcasted_iota(jnp.int32, s.shape, 0) < N_META, s, NEG_BIG)


def _score_shift(qt, kmax):
    qf = qt.astype(F32)
    return jnp.sqrt(jnp.sum(qf * qf, axis=0, keepdims=True)) * (kmax[0:1, 0:1] * BOUND_SLACK)


def _flash_cols_shifted(qt, s_meta, shift, k_ref, vt_ref, vmeta_t, acc_scr):
    n_chunks, tk = vt_ref.shape[2], vt_ref.shape[4]
    group = CHUNKS_PER_ITER

    def weights(s):
        return jnp.exp2(s - shift).astype(BF16)

    def chunk_pv(c):
        off = c * tk if isinstance(c, int) else pl.multiple_of(c * tk, tk)
        s = jnp.dot(k_ref[0, 0, pl.ds(off, tk), :], qt, preferred_element_type=F32)
        return jnp.dot(vt_ref[0, 0, c], weights(s), preferred_element_type=F32)

    acc_scr[...] = jnp.zeros_like(acc_scr)

    def body(j, carry):
        total = chunk_pv(group * j)
        for u in range(1, group):
            total = total + chunk_pv(group * j + u)
        acc_scr[...] += total
        return carry

    n_iter = n_chunks // group if n_chunks >= 2 * group else 0
    lax.fori_loop(0, n_iter, body, 0)
    for c in range(group * n_iter, n_chunks):
        acc_scr[...] += chunk_pv(c)
    return acc_scr[...] + jnp.dot(vmeta_t, weights(s_meta), preferred_element_type=F32)


def _flash_cols_online(qt, s_meta, k_ref, vt_ref, vmeta_t, s_scrs, cmax_scrs, m_scr, acc_scr):
    n_chunks, tk = vt_ref.shape[2], vt_ref.shape[4]
    r = qt.shape[1]
    group = ONLINE_CHUNKS_PER_ITER
    assert group % 2 == 0

    def bcast(x):
        return jnp.broadcast_to(x, (SUBLANES, r))

    def scores(c):
        off = c * tk if isinstance(c, int) else pl.multiple_of(c * tk, tk)
        return jnp.dot(k_ref[0, 0, pl.ds(off, tk), :], qt, preferred_element_type=F32)

    def put(s, slot, rows):
        s_scrs[slot][0:rows, :] = s
        cmax_scrs[slot][...] = bcast(jnp.max(s, axis=0, keepdims=True))

    def fold(slot, rows, vt, first=False):
        cmax = cmax_scrs[slot][0:1, :]
        m_new = cmax if first else jnp.maximum(m_scr[0:1, :], cmax)
        p = jnp.exp2(s_scrs[slot][0:rows, :] - m_new)
        pv = jnp.dot(vt, p.astype(BF16), preferred_element_type=F32)
        if first:
            acc_scr[...] = pv
        else:
            acc_scr[...] = jnp.exp2(m_scr[0:1, :] - m_new) * acc_scr[...] + pv
        m_scr[...] = bcast(m_new)

    put(s_meta, 1, META_PAD)
    put(scores(0), 0, tk)
    fold(1, META_PAD, vmeta_t, first=True)

    def step(c, parity, last):
        if not last:
            put(scores(c + 1), 1 - parity, tk)
        fold(parity, tk, vt_ref[0, 0, c])

    n_iter = (n_chunks - 1) // group

    def body(j, carry):
        for u in range(group):
            step(group * j + u, u % 2, False)
        return carry

    lax.fori_loop(0, n_iter, body, 0)
    for c in range(group * n_iter, n_chunks):
        step(c, c % 2, c == n_chunks - 1)
    return acc_scr[...]


def _normalised(acc, dv):
    return acc[0:dv, :] * (1.0 / acc[dv:dv + 1, :])


def _flash_cols(qt, k_ref, vt_ref, kmeta, vmeta_t, kmax, finish,
                s0_scr, s1_scr, c0_scr, c1_scr, m_scr, acc_scr):
    s_meta = _masked_meta_scores(kmeta, qt)
    shift = _score_shift(qt, kmax)
    bounded = jnp.max(shift) <= SCORE_BOUND

    @pl.when(bounded)
    def _():
        finish(_flash_cols_shifted(qt, s_meta, shift, k_ref, vt_ref, vmeta_t, acc_scr))

    @pl.when(jnp.logical_not(bounded))
    def _():
        finish(_flash_cols_online(qt, s_meta, k_ref, vt_ref, vmeta_t,
                                  (s0_scr, s1_scr), (c0_scr, c1_scr), m_scr, acc_scr))


def _da_attn_kernel(qt_ref, k_ref, vt_ref, kmeta_ref, vmeta_ref, kmax_ref, lq1_ref, lk1_ref, lq2_ref, lk2_ref,
                    g_ref, o_ref, *scratch):
    tq = o_ref.shape[1]
    qt = jnp.concatenate([qt_ref[0, 0, 0, 0], qt_ref[0, 0, 0, 1]], axis=1)

    def finish(acc):
        at = _normalised(acc, DA_V_DIM)
        lam = (jnp.exp(jnp.sum(lq1_ref[...] * lk1_ref[...], axis=1, keepdims=True))
               - jnp.exp(jnp.sum(lq2_ref[...] * lk2_ref[...], axis=1, keepdims=True)) + LAMBDA_INIT)
        ot = at[:, :tq] - lam * at[:, tq:]
        inv = lax.rsqrt(jnp.mean(ot * ot, axis=0, keepdims=True) + RMS_EPS)
        o_ref[0] = ((ot * inv).T * g_ref[...] * (1.0 - LAMBDA_INIT)).astype(BF16)

    _flash_cols(qt, k_ref, vt_ref, kmeta_ref[0, 0], vmeta_ref[0, 0], kmax_ref[0, 0], finish, *scratch)


def _mla_attn_kernel(qt_ref, k_ref, vt_ref, kmeta_ref, vmeta_ref, kmax_ref, o_ref, *scratch):
    def finish(acc):
        o_ref[0] = _normalised(acc, MLA_V).T.astype(BF16)

    _flash_cols(qt_ref[0, 0, 0], k_ref, vt_ref, kmeta_ref[0, 0], vmeta_ref[0, 0], kmax_ref[0, 0], finish, *scratch)


def _attn_scratch(cols, tk):
    stat = pltpu.VMEM((SUBLANES, cols), F32)
    return [pltpu.VMEM((tk, cols), F32), pltpu.VMEM((tk, cols), F32), stat, stat, stat,
            pltpu.VMEM((V_ROWS, cols), F32)]


def _small_spec(shape, index_map):
    return pl.BlockSpec(shape, index_map, pipeline_mode=pl.Buffered(1))


def _kmax_spec():
    return _small_spec((1, 1, 1, LANES), lambda b, h, i: (b, h, 0, 0))


def _attn_vmem(cols, tk, s, dk, dv):
    return (2 * s * (dk + dv) * 2 + 2 * dk * cols * 2 + 2 * tk * cols * 4 + dv * cols * 4
            + 4 * tk * cols * 4 + 2 * cols * dv * 2) / MIB + 8


def _da_attn(qdat, kda, vdat, kmeta, vmeta_t, kmax, lam_vecs, g):
    nb, nh, nq, _, _, tq = qdat.shape
    cols = 2 * tq
    s = kda.shape[2]
    nc, tk = vdat.shape[2], vdat.shape[4]
    vec_spec = _small_spec((1, DA_HEAD_DIM), lambda b, h, i: (0, 0))
    meta_spec = _small_spec((1, 1, META_PAD, LANES), lambda b, h, i: (0, h, 0, 0))
    vmeta_spec = _small_spec((1, 1, V_ROWS, META_PAD), lambda b, h, i: (0, h, 0, 0))
    return pl.pallas_call(
        _da_attn_kernel,
        grid=(nb, nh, nq),
        in_specs=[
            pl.BlockSpec((1, 1, 1, 2, LANES, tq), lambda b, h, i: (b, h, i, 0, 0, 0)),
            pl.BlockSpec((1, 1, s, LANES), lambda b, h, i: (b, h, 0, 0)),
            pl.BlockSpec((1, 1, nc, V_ROWS, tk), lambda b, h, i: (b, h, 0, 0, 0)),
            meta_spec, vmeta_spec, _kmax_spec(),
            vec_spec, vec_spec, vec_spec, vec_spec,
            _small_spec((1, DA_V_DIM), lambda b, h, i: (0, 0)),
        ],
        out_specs=pl.BlockSpec((1, tq, DA_V_DIM), lambda b, h, i: (b, i, h)),
        out_shape=jax.ShapeDtypeStruct((nb, s, DA_WIDTH), BF16),
        scratch_shapes=_attn_scratch(cols, tk),
        compiler_params=_cparams(3, _attn_vmem(cols, tk, s, LANES, DA_V_DIM)),
        name="da_attn",
    )(qdat, kda, vdat, kmeta, vmeta_t, kmax, *lam_vecs, g)


def _mla_attn(qmt, km, vmt, kmeta, vmeta_t, kmax):
    nb, nh, nq, _, tq = qmt.shape
    s = km.shape[2]
    nc, tk = vmt.shape[2], vmt.shape[4]
    return pl.pallas_call(
        _mla_attn_kernel,
        grid=(nb, nh, nq),
        in_specs=[
            pl.BlockSpec((1, 1, 1, MLA_QK_PAD, tq), lambda b, h, i: (b, h, i, 0, 0)),
            pl.BlockSpec((1, 1, s, MLA_QK_PAD), lambda b, h, i: (b, h, 0, 0)),
            pl.BlockSpec((1, 1, nc, V_ROWS, tk), lambda b, h, i: (b, h, 0, 0, 0)),
            _small_spec((1, 1, META_PAD, MLA_QK_PAD), lambda b, h, i: (0, h, 0, 0)),
            _small_spec((1, 1, V_ROWS, META_PAD), lambda b, h, i: (0, h, 0, 0)),
            _kmax_spec(),
        ],
        out_specs=pl.BlockSpec((1, tq, MLA_V), lambda b, h, i: (b, i, h)),
        out_shape=jax.ShapeDtypeStruct((nb, s, MLA_WIDTH), BF16),
        scratch_shapes=_attn_scratch(tq, tk),
        compiler_params=_cparams(3, _attn_vmem(tq, tk, s, MLA_QK_PAD, MLA_V)),
        name="mla_attn",
    )(qmt, km, vmt, kmeta, vmeta_t, kmax)


def _outproj_ln_kernel(h_ref, oa_ref, om_ref, wo_ref, g_ref, b_ref, o_ref):
    tm = h_ref.shape[0]
    sub = tm // OUTPROJ_SUBBLOCKS if tm % (OUTPROJ_SUBBLOCKS * BF16_SUBLANES) == 0 else tm
    for r0 in range(0, tm, sub):
        rows = pl.ds(r0, sub)
        mix = (jnp.dot(oa_ref[rows, :], wo_ref[0:DA_WIDTH, :], preferred_element_type=F32)
               + jnp.dot(om_ref[rows, :], wo_ref[DA_WIDTH:DA_WIDTH + MLA_WIDTH, :], preferred_element_type=F32))
        o_ref[rows, :] = _layer_norm(ALPHA * h_ref[rows, :] + mix, g_ref[...], b_ref[...])


def _outproj_ln(h, oa, om, wo, g, b, *, tm=512):
    m, d = h.shape
    tm = min(tm, m)
    assert m % tm == 0
    vmem = (2 * 2 * tm * d * 4 + 2 * 2 * tm * DA_WIDTH * 2 + wo.size * 2 + 3 * tm * d * 4) / MIB + 6
    return pl.pallas_call(
        _outproj_ln_kernel,
        grid=(m // tm,),
        in_specs=[
            pl.BlockSpec((tm, d), lambda i: (i, 0)),
            pl.BlockSpec((tm, DA_WIDTH), lambda i: (i, 0)),
            pl.BlockSpec((tm, MLA_WIDTH), lambda i: (i, 0)),
            _resident(wo.shape), _resident(g.shape), _resident(b.shape),
        ],
        out_specs=pl.BlockSpec((tm, d), lambda i: (i, 0)),
        out_shape=jax.ShapeDtypeStruct((m, d), F32),
        compiler_params=_cparams(1, vmem),
        name="outproj_ln",
    )(h, oa, om, wo, g, b)


def _rope_lane_tables(n_pos, rot_dim, period):
    half = rot_dim // 2
    inv = 1.0 / (ROPE_THETA ** (jnp.arange(0, rot_dim, 2, dtype=F32) / rot_dim))
    ang = jnp.arange(n_pos, dtype=F32)[:, None] * inv[None, :]
    cos, sin = jnp.cos(ang), jnp.sin(ang)
    within = jnp.arange(LANES) % period
    idx = within % half
    c = jnp.where(within < rot_dim, cos[:, idx], 1.0)
    sp = jnp.where((within >= half) & (within < rot_dim), sin[:, idx], 0.0)
    sm = jnp.where(within < half, -sin[:, idx], 0.0)
    return c, sp, sm


def _prep_weights(p):
    w = {}
    for name in ("ffn1_w_gate", "ffn1_w_up", "ffn2_w_gate", "ffn2_w_up", "w_o", "w_ukv"):
        w[name] = p[name][0].astype(BF16)
    for name in ("ffn1_w_down", "ffn2_w_down"):
        w[name] = (p[name][0] * 0.5).astype(BF16)
    w_in = p["w_in"][0]
    w["w_in"] = jnp.pad(w_in, ((0, 0), (0, LANES - MLA_ROPE))).astype(BF16)
    w_uq = p["w_uq"][0].reshape(MLA_Q_RANK, MLA_HEADS, MLA_NOPE + MLA_ROPE)
    w_uq = jnp.pad(w_uq, ((0, 0), (0, 0), (0, MLA_QK_PAD - MLA_NOPE - MLA_ROPE)))
    w["w_uq"] = w_uq.reshape(MLA_Q_RANK, MLA_HEADS * MLA_QK_PAD).astype(BF16)
    for name in ("ln1_g", "ln1_b", "ln2_g", "ln2_b", "ln3_g", "ln3_b", "da_subln_g", "mla_q_norm_g", "mla_kv_norm_g",
                 "da_lambda_q1", "da_lambda_k1", "da_lambda_q2", "da_lambda_k2"):
        w[name] = p[name].astype(F32)
    return w


def _mixer_inputs(h, nb, s, w, tabs_da, tabs_mla):
    return _inproj(h, nb, s, w["w_in"], w["mla_q_norm_g"], w["w_uq"], w["mla_kv_norm_g"], w["w_ukv"],
                   tabs_da, tabs_mla)


def _trunk(x, meta_kv, w, tabs_da, tabs_mla):
    nb, s, d = x.shape
    h1 = _ffn_ln(x.reshape(nb * s, d), w["ffn1_w_gate"], w["ffn1_w_up"], w["ffn1_w_down"], w["ln1_g"], w["ln1_b"])
    qdat, kda, vdat, qmt, km, vmt, knorm2 = _mixer_inputs(h1, nb, s, w, tabs_da, tabs_mla)
    kda_m, vdat_m, km_m, vmt_m, knorm2_m = meta_kv
    knorm2 = jnp.max(knorm2.reshape(nb, -1, DA_HEADS + MLA_HEADS, LANES), axis=1)
    kmax = jnp.sqrt(jnp.maximum(knorm2, knorm2_m))[:, :, None, :]
    lam_vecs = (w["da_lambda_q1"], w["da_lambda_k1"], w["da_lambda_q2"], w["da_lambda_k2"])
    oa = _da_attn(qdat, kda, vdat, kda_m, vdat_m, kmax[:, :DA_HEADS], lam_vecs, w["da_subln_g"])
    om = _mla_attn(qmt, km, vmt, km_m, vmt_m, kmax[:, DA_HEADS:])
    h2 = _outproj_ln(h1, oa.reshape(nb * s, DA_WIDTH), om.reshape(nb * s, MLA_WIDTH), w["w_o"], w["ln2_g"], w["ln2_b"])
    y = _ffn_ln(h2, w["ffn2_w_gate"], w["ffn2_w_up"], w["ffn2_w_down"], w["ln3_g"], w["ln3_b"])
    return y.reshape(nb, s, d)


def kernel(x_prompt, x_sample, meta_tokens, ffn1_w_gate, ffn1_w_up, ffn1_w_down, ln1_g, ln1_b, w_in, da_lambda_q1, da_lambda_k1, da_lambda_q2, da_lambda_k2, da_subln_g, mla_q_norm_g, w_uq, mla_kv_norm_g, w_ukv, w_o, ln2_g, ln2_b, ffn2_w_gate, ffn2_w_up, ffn2_w_down, ln3_g, ln3_b):
    p = dict(ffn1_w_gate=ffn1_w_gate, ffn1_w_up=ffn1_w_up, ffn1_w_down=ffn1_w_down, ln1_g=ln1_g, ln1_b=ln1_b,
             w_in=w_in, da_lambda_q1=da_lambda_q1, da_lambda_k1=da_lambda_k1, da_lambda_q2=da_lambda_q2,
             da_lambda_k2=da_lambda_k2, da_subln_g=da_subln_g, mla_q_norm_g=mla_q_norm_g, w_uq=w_uq,
             mla_kv_norm_g=mla_kv_norm_g, w_ukv=w_ukv, w_o=w_o, ln2_g=ln2_g, ln2_b=ln2_b,
             ffn2_w_gate=ffn2_w_gate, ffn2_w_up=ffn2_w_up, ffn2_w_down=ffn2_w_down, ln3_g=ln3_g, ln3_b=ln3_b)
    w = _prep_weights(p)
    s_max = max(x_prompt.shape[1], x_sample.shape[1])
    tabs_da = _rope_lane_tables(N_META + s_max, DA_ROT, DA_HEAD_DIM)
    tabs_mla = _rope_lane_tables(N_META + s_max, MLA_ROPE, LANES)

    hm = _ffn_ln(meta_tokens.astype(F32), w["ffn1_w_gate"], w["ffn1_w_up"], w["ffn1_w_down"], w["ln1_g"], w["ln1_b"])
    hm = jnp.pad(hm, ((0, PROJ_ROWS - N_META), (0, 0)))
    _, kda_m, vdat_m, _, km_m, vmt_m, knorm2_m = _mixer_inputs(
        hm, 1, PROJ_ROWS, w, [t[:PROJ_ROWS] for t in tabs_da], [t[:PROJ_ROWS] for t in tabs_mla])
    meta_kv = (kda_m[:, :, :META_PAD], vdat_m[:, :, 0, :, :META_PAD], km_m[:, :, :META_PAD],
               vmt_m[:, :, 0, :, :META_PAD], knorm2_m)

    seq_da = [t[N_META:] for t in tabs_da]
    seq_mla = [t[N_META:] for t in tabs_mla]
    y_prompt = _trunk(x_prompt, meta_kv, w, seq_da, seq_mla)
    y_sample = _trunk(x_sample, meta_kv, w, seq_da, seq_mla)
    return (y_prompt, y_sample)
```
